```python
import math
import jax, jax.numpy as jnp
from jax import lax
import numpy as np

D_MODEL = 1024
BATCH = 4
SEQ = 4096
DEPTH = 4

RWKV_HEADS = 4
RWKV_HEAD_DIM = 64
RWKV_W = RWKV_HEADS * RWKV_HEAD_DIM
DECAY_LORA = 64
AAA_LORA = 64
GATE_LORA = 128
RWKV_IN = 3 * RWKV_W + DECAY_LORA + AAA_LORA + GATE_LORA
S5_GROUPS = 16
S5_GROUP_CH = 16
S5_W = S5_GROUPS * S5_GROUP_CH
S5_STATE = 64
FOX_HEADS = 8
FOX_HEAD_DIM = 64
FOX_W = FOX_HEADS * FOX_HEAD_DIM
FOX_IN = 4 * FOX_W + FOX_HEADS
N_IN = RWKV_IN + S5_W + FOX_IN
N_MEM = 256
XATTN_HEADS = 4
XATTN_HEAD_DIM = D_MODEL // XATTN_HEADS
D_FF = 4 * D_MODEL
Q_BLOCK = 128
NORM_EPS = 1e-6
RWKV_LN_EPS = 64e-5

kernel_name = 'hymba_rwkv7_s5_fox_hybrid'


def rmsnorm(x, g, eps=NORM_EPS):
    xf = x.astype(jnp.float32)
    y = xf * lax.rsqrt(jnp.mean(xf * xf, axis=-1, keepdims=True) + eps)
    return (y * g.astype(jnp.float32)).astype(x.dtype)


def token_shift(p):
    return jnp.pad(p, ((0, 0), (1, 0), (0, 0)))[:, :-1]


def rwkv7_time_mix(p, mu, w0, w2, a0, a2, g2, k_k, k_a, r_k, ln_w, ln_b):
    B, T, _ = p.shape
    H, N = RWKV_HEADS, RWKV_HEAD_DIM
    p = p.astype(jnp.float32)
    p = p + (token_shift(p) - p) * mu
    r, w_l, k, v, a_l, g_l = jnp.split(
        p, [RWKV_W, RWKV_W + DECAY_LORA, 2 * RWKV_W + DECAY_LORA,
            3 * RWKV_W + DECAY_LORA, 3 * RWKV_W + DECAY_LORA + AAA_LORA], axis=-1)
    w = -jax.nn.softplus(-(w0 + jnp.tanh(w_l) @ w2)) - 0.5
    decay = jnp.exp(-jnp.exp(w))
    a = jax.nn.sigmoid(a0 + a_l @ a2)
    g = jax.nn.sigmoid(g_l) @ g2
    hs = lambda z: z.reshape(B, T, H, N)
    kk = hs(k * k_k)
    kk = kk / jnp.maximum(jnp.sqrt(jnp.sum(kk * kk, axis=-1, keepdims=True)), 1e-12)
    k = k * (1.0 + (a - 1.0) * k_a)
    r, decay, k, v, a = hs(r), hs(decay), hs(k), hs(v), hs(a)

    def step(S, inp):
        r_t, w_t, k_t, v_t, a_t, b_t = inp
        sa = jnp.einsum('bhij,bhj->bhi', S, a_t)
        S = S * w_t[:, :, None, :] + sa[..., None] * b_t[:, :, None, :] + v_t[..., None] * k_t[:, :, None, :]
        y_t = jnp.einsum('bhij,bhj->bhi', S, r_t)
        return S, y_t

    seq_in = tuple(z.transpose(1, 0, 2, 3) for z in (r, decay, k, v, -kk, kk * a))
    S0 = jnp.zeros((B, H, N, N), jnp.float32)
    _, y = lax.scan(step, S0, seq_in)
    y = y.transpose(1, 0, 2, 3)
    mean = jnp.mean(y, axis=-1, keepdims=True)
    var = jnp.mean(jnp.square(y - mean), axis=-1, keepdims=True)
    y = ((y - mean) * lax.rsqrt(var + RWKV_LN_EPS)).reshape(B, T, RWKV_W) * ln_w + ln_b
    y = y + (jnp.sum(r * k * r_k, axis=-1, keepdims=True) * v).reshape(B, T, RWKV_W)
    return y * g


def s5_mix(u, lam_re, lam_im, log_dt, b_re, b_im, c_re, c_im, d, w_glu, b_glu):
    B, T, _ = u.shape
    u = u.astype(jnp.float32)
    ug = u.reshape(B, T, S5_GROUPS, S5_GROUP_CH)
    dt = jnp.exp(log_dt)[:, None]
    mag = jnp.exp(lam_re * dt)
    ab_re = mag * jnp.cos(lam_im * dt)
    ab_im = mag * jnp.sin(lam_im * dt)
    den = lam_re * lam_re + lam_im * lam_im
    nr = ab_re - 1.0
    coef_re = (nr * lam_re + ab_im * lam_im) / den
    coef_im = (ab_im * lam_re - nr * lam_im) / den
    bb_re = coef_re[..., None] * b_re - coef_im[..., None] * b_im
    bb_im = coef_re[..., None] * b_im + coef_im[..., None] * b_re
    bu_re = jnp.einsum('btgh,gph->btgp', ug, bb_re)
    bu_im = jnp.einsum('btgh,gph->btgp', ug, bb_im)
    a_re = jnp.broadcast_to(ab_re, bu_re.shape)
    a_im = jnp.broadcast_to(ab_im, bu_im.shape)

    def combine(e1, e2):
        a1r, a1i, b1r, b1i = e1
        a2r, a2i, b2r, b2i = e2
        return (a1r * a2r - a1i * a2i, a1r * a2i + a1i * a2r,
                a2r * b1r - a2i * b1i + b2r, a2r * b1i + a2i * b1r + b2i)

    _, _, s_re, s_im = lax.associative_scan(combine, (a_re, a_im, bu_re, bu_im), axis=1)
    y = jnp.einsum('btgp,ghp->btgh', s_re, c_re) - jnp.einsum('btgp,ghp->btgh', s_im, c_im)
    y = y.reshape(B, T, S5_W) + d * u
    y = jax.nn.gelu(y)
    return y * jax.nn.sigmoid(y @ w_glu + b_glu)


def forgetting_attention(q, k, v, c):
    B, T, H, Dh = q.shape
    nb = T // Q_BLOCK
    scale = 1.0 / math.sqrt(Dh)
    c_k = c.transpose(0, 2, 1)
    k_pos = jnp.arange(T)
    q_blocks = q.reshape(B, nb, Q_BLOCK, H, Dh).transpose(1, 0, 2, 3, 4)
    c_blocks = c_k.reshape(B, H, nb, Q_BLOCK).transpose(2, 0, 1, 3)

    def one_block(args):
        q_i, c_i, i = args
        s = jnp.einsum('bqhd,bkhd->bhqk', q_i, k) * scale
        s = s + c_i[..., :, None] - c_k[:, :, None, :]
        q_pos = i * Q_BLOCK + jnp.arange(Q_BLOCK)
        s = jnp.where(k_pos[None, :] <= q_pos[:, None], s, -jnp.inf)
        prob = jax.nn.softmax(s, axis=-1)
        return jnp.einsum('bhqk,bkhd->bqhd', prob, v)

    o = lax.map(one_block, (q_blocks, c_blocks, jnp.arange(nb)))
    return o.transpose(1, 0, 2, 3, 4).reshape(B, T, H, Dh)


def fox_mix(p, q_g, k_g, b_f):
    B, T, _ = p.shape
    q, k, v, og, f_l = jnp.split(p, [FOX_W, 2 * FOX_W, 3 * FOX_W, 4 * FOX_W], axis=-1)
    hs = lambda z: z.reshape(B, T, FOX_HEADS, FOX_HEAD_DIM).astype(jnp.float32)
    q = rmsnorm(hs(q), q_g)
    k = rmsnorm(hs(k), k_g)
    log_f = jax.nn.log_sigmoid(f_l.astype(jnp.float32) + b_f)
    c = jnp.cumsum(log_f, axis=1)
    o = forgetting_attention(q, k, hs(v), c)
    return o.reshape(B, T, FOX_W) * jax.nn.sigmoid(og.astype(jnp.float32))


def cross_attention(hn, w_q, w_o, mem_k, mem_v):
    B, T, _ = hn.shape
    q = (hn @ w_q).reshape(B, T, XATTN_HEADS, XATTN_HEAD_DIM)
    s = jnp.einsum('bthd,bmhd->bhtm', q.astype(jnp.float32), mem_k.astype(jnp.float32)) * (1.0 / math.sqrt(XATTN_HEAD_DIM))
    prob = jax.nn.softmax(s, axis=-1)
    o = jnp.einsum('bhtm,bmhd->bthd', prob.astype(hn.dtype), mem_v).reshape(B, T, D_MODEL)
    return o @ w_o


def setup_inputs(seed: int = 0) -> dict:
    key = jax.random.key(seed)
    ks = iter(jax.random.split(key, 64))
    nrm = lambda shape, scale: scale * jax.random.normal(next(ks), shape, jnp.float32)
    unif = lambda shape, lo, hi: jax.random.uniform(next(ks), shape, jnp.float32, lo, hi)
    gain = lambda shape: 1.0 + nrm(shape, 0.02)
    L = DEPTH
    n_idx = jnp.arange(S5_STATE, dtype=jnp.float32)
    return {
        'x': nrm((BATCH, SEQ, D_MODEL), 1.0),
        'mem': nrm((BATCH, N_MEM, D_MODEL), 1.0),
        'mem_norm_g': gain((D_MODEL,)),
        'w_mem_kv': nrm((D_MODEL, 2 * D_MODEL), D_MODEL ** -0.5),
        'mix_norm_g': gain((L, D_MODEL)),
        'w_in': nrm((L, D_MODEL, N_IN), D_MODEL ** -0.5),
        'rwkv_mu': unif((L, RWKV_IN), 0.0, 1.0),
        'rwkv_w0': unif((L, RWKV_W), -6.5, -1.5),
        'rwkv_w2': nrm((L, DECAY_LORA, RWKV_W), 0.1 * DECAY_LORA ** -0.5),
        'rwkv_a0': nrm((L, RWKV_W), 0.1),
        'rwkv_a2': nrm((L, AAA_LORA, RWKV_W), 0.1 * AAA_LORA ** -0.5),
        'rwkv_g2': nrm((L, GATE_LORA, RWKV_W), GATE_LORA ** -0.5),
        'rwkv_k_k': 0.85 + nrm((L, RWKV_W), 0.02),
        'rwkv_k_a': 1.0 + nrm((L, RWKV_W), 0.02),
        'rwkv_r_k': -0.04 + nrm((L, RWKV_HEADS, RWKV_HEAD_DIM), 0.01),
        'rwkv_ln_w': gain((L, RWKV_W)),
        'rwkv_ln_b': nrm((L, RWKV_W), 0.02),
        's5_lam_re': -0.5 + nrm((L, S5_GROUPS, S5_STATE), 0.01),
        's5_lam_im': jnp.pi * n_idx + nrm((L, S5_GROUPS, S5_STATE), 0.01),
        's5_log_dt': unif((L, S5_GROUPS), math.log(1e-3), math.log(1e-1)),
        's5_b_re': nrm((L, S5_GROUPS, S5_STATE, S5_GROUP_CH), (2 * S5_GROUP_CH) ** -0.5),
        's5_b_im': nrm((L, S5_GROUPS, S5_STATE, S5_GROUP_CH), (2 * S5_GROUP_CH) ** -0.5),
        's5_c_re': nrm((L, S5_GROUPS, S5_GROUP_CH, S5_STATE), (2 * S5_STATE) ** -0.5),
        's5_c_im': nrm((L, S5_GROUPS, S5_GROUP_CH, S5_STATE), (2 * S5_STATE) ** -0.5),
        's5_d': nrm((L, S5_W), 1.0),
        's5_w_glu': nrm((L, S5_W, S5_W), S5_W ** -0.5),
        's5_b_glu': nrm((L, S5_W), 0.02),
        's5_out_g': gain((L, S5_W)),
        'fox_q_g': gain((L, FOX_HEAD_DIM)),
        'fox_k_g': gain((L, FOX_HEAD_DIM)),
        'fox_b_f': unif((L, FOX_HEADS), 1.0, 4.0),
        'fox_out_g': gain((L, FOX_W)),
        'w_out': nrm((L, D_MODEL, D_MODEL), D_MODEL ** -0.5),
        'xattn_norm_g': gain((L, D_MODEL)),
        'w_xq': nrm((L, D_MODEL, D_MODEL), D_MODEL ** -0.5),
        'w_xo': nrm((L, D_MODEL, D_MODEL), D_MODEL ** -0.5),
        'ffn_norm_g': gain((L, D_MODEL)),
        'w_ffn1': nrm((L, D_MODEL, D_FF), D_MODEL ** -0.5),
        'w_ffn2': nrm((L, D_FF, D_MODEL), D_FF ** -0.5),
        'final_norm_g': gain((D_MODEL,)),
    }


def reference(x, mem, mem_norm_g, w_mem_kv, mix_norm_g, w_in, rwkv_mu, rwkv_w0, rwkv_w2,
              rwkv_a0, rwkv_a2, rwkv_g2, rwkv_k_k, rwkv_k_a, rwkv_r_k, rwkv_ln_w, rwkv_ln_b,
              s5_lam_re, s5_lam_im, s5_log_dt, s5_b_re, s5_b_im, s5_c_re, s5_c_im, s5_d,
              s5_w_glu, s5_b_glu, s5_out_g, fox_q_g, fox_k_g, fox_b_f, fox_out_g, w_out,
              xattn_norm_g, w_xq, w_xo, ffn_norm_g, w_ffn1, w_ffn2, final_norm_g):
    B, T, _ = x.shape
    M = mem.shape[1]
    act_dtype = x.dtype
    mem_kv = rmsnorm(mem, mem_norm_g) @ w_mem_kv
    mem_k, mem_v = jnp.split(mem_kv, 2, axis=-1)
    mem_k = mem_k.reshape(B, M, XATTN_HEADS, XATTN_HEAD_DIM)
    mem_v = mem_v.reshape(B, M, XATTN_HEADS, XATTN_HEAD_DIM)
    h = x
    for l in range(DEPTH):
        p = rmsnorm(h, mix_norm_g[l]) @ w_in[l]
        p_rwkv, p_s5, p_fox = jnp.split(p, [RWKV_IN, RWKV_IN + S5_W], axis=-1)
        y_rwkv = rwkv7_time_mix(p_rwkv, rwkv_mu[l], rwkv_w0[l], rwkv_w2[l], rwkv_a0[l], rwkv_a2[l],
                                rwkv_g2[l], rwkv_k_k[l], rwkv_k_a[l], rwkv_r_k[l], rwkv_ln_w[l], rwkv_ln_b[l])
        y_s5 = rmsnorm(s5_mix(p_s5, s5_lam_re[l], s5_lam_im[l], s5_log_dt[l], s5_b_re[l], s5_b_im[l],
                              s5_c_re[l], s5_c_im[l], s5_d[l], s5_w_glu[l], s5_b_glu[l]), s5_out_g[l])
        y_fox = rmsnorm(fox_mix(p_fox, fox_q_g[l], fox_k_g[l], fox_b_f[l]), fox_out_g[l])
        y = jnp.concatenate([y_rwkv.astype(act_dtype), y_s5.astype(act_dtype), y_fox.astype(act_dtype)], axis=-1)
        h = h + y @ w_out[l]
        h = h + cross_attention(rmsnorm(h, xattn_norm_g[l]), w_xq[l], w_xo[l], mem_k, mem_v)
        z = rmsnorm(h, ffn_norm_g[l]) @ w_ffn1[l]
        h = h + jnp.square(jax.nn.relu(z)) @ w_ffn2[l]
    return rmsnorm(h, final_norm_g)
```

```python
import functools
import math

import jax
import jax.numpy as jnp
from jax import lax
from jax.experimental import pallas as pl
from jax.experimental.pallas import tpu as pltpu

F32 = jnp.float32
BF16 = jnp.bfloat16

D_MODEL = 1024
RWKV_HEADS = 4
HEAD_DIM = 64
RWKV_W = RWKV_HEADS * HEAD_DIM
DECAY_LORA = 64
AAA_LORA = 64
GATE_LORA = 128
RWKV_IN = 3 * RWKV_W + DECAY_LORA + AAA_LORA + GATE_LORA
S5_GROUPS = 16
S5_GROUP_CH = 16
S5_W = S5_GROUPS * S5_GROUP_CH
S5_STATE = 64
FOX_HEADS = 8
FOX_W = FOX_HEADS * HEAD_DIM
XATTN_HEADS = 4
XATTN_HEAD_DIM = D_MODEL // XATTN_HEADS
NORM_EPS = 1e-6
RWKV_LN_EPS = 64e-5

LANES = 128
RWKV_CHUNK = 64
S5_CHUNK = 16
FOX_AUG = 6
VMEM_LIMIT = 56 * 1024 * 1024

NEG_BIG = -1e30


def _cparams(sem):
    return pltpu.CompilerParams(dimension_semantics=sem, vmem_limit_bytes=VMEM_LIMIT)


def _bdot(a, b):
    return jnp.dot(a.astype(BF16), b.astype(BF16), preferred_element_type=F32)


def _bdot_nt(a, b):
    return lax.dot_general(a.astype(BF16), b.astype(BF16), (((1,), (1,)), ((), ())),
                           preferred_element_type=F32)


def _split2(x):
    hi = x.astype(BF16)
    lo = (x - hi.astype(F32)).astype(BF16)
    return hi, lo


def _split3(x):
    hi = x.astype(BF16)
    r1 = x - hi.astype(F32)
    mid = r1.astype(BF16)
    lo = (r1 - mid.astype(F32)).astype(BF16)
    return hi, mid, lo


def _dot_exact_lhs(a_bf16, x):
    hi, mid, lo = _split3(x)
    d = lambda p: jnp.dot(a_bf16, p, preferred_element_type=F32)
    return d(hi) + d(mid) + d(lo)


def _dot_exact_rhs(x, b_bf16):
    hi, lo = _split2(x)
    d = lambda p: jnp.dot(p, b_bf16, preferred_element_type=F32)
    return d(hi) + d(lo)


def _mm3(a, b):
    ah, al = _split2(a)
    bh, bl = _split2(b)
    d = lambda p, q: jnp.dot(p, q, preferred_element_type=F32)
    return d(ah, bh) + d(ah, bl) + d(al, bh)


def _sigmoid(x):
    return 1.0 / (1.0 + jnp.exp(-x))


def _softplus(x):
    return jnp.maximum(x, 0.0) + jnp.log(1.0 + jnp.exp(-jnp.abs(x)))


def _rms(x, g, eps=NORM_EPS):
    return x * lax.rsqrt(jnp.mean(x * x, axis=-1, keepdims=True) + eps) * g


def _iota(shape, dim):
    return lax.broadcasted_iota(jnp.int32, shape, dim)


def _vmem_full():
    return pl.BlockSpec(memory_space=pltpu.VMEM)


def _norm_matmul_kernel(x_ref, g_ref, w_ref, *out_refs, col_ranges, col_chunk):
    xn = _rms(x_ref[...], g_ref[...]).astype(BF16)
    for o_ref, (c0, c1) in zip(out_refs, col_ranges):
        for cc in range(c0, c1, col_chunk):
            ce = min(cc + col_chunk, c1)
            o_ref[:, cc - c0:ce - c0] = jnp.dot(
                xn, w_ref[:, cc:ce], preferred_element_type=F32).astype(o_ref.dtype)


def _norm_matmul(x, g, w, col_ranges, out_dtypes, tm=512):
    n, d = x.shape
    tm = min(tm, n)
    assert n % tm == 0
    kern = functools.partial(_norm_matmul_kernel, col_ranges=tuple(col_ranges), col_chunk=512)
    return pl.pallas_call(
        kern,
        grid=(n // tm,),
        in_specs=[pl.BlockSpec((tm, d), lambda i: (i, 0)),
                  pl.BlockSpec((1, d), lambda i: (0, 0)),
                  _vmem_full()],
        out_specs=[pl.BlockSpec((tm, c1 - c0), lambda i: (i, 0)) for (c0, c1) in col_ranges],
        out_shape=[jax.ShapeDtypeStruct((n, c1 - c0), dt) for (c0, c1), dt in zip(col_ranges, out_dtypes)],
        compiler_params=_cparams(("parallel",)),
        name="norm_matmul",
    )(x, g.reshape(1, d), w)


def _rwkv_prep_kernel(p_ref, mu_ref, w0_ref, a0_ref, kk_ref, ka_ref, wwa_ref, g2_ref, e_ref,
                      r_out, k_out, v_out, an_out, b_out, ld_out, g_out, carry_ref):
    W = RWKV_W

    @pl.when(pl.program_id(1) == 0)
    def _():
        carry_ref[...] = jnp.zeros_like(carry_ref)

    p = p_ref[0]
    tb = p.shape[0]
    prev = carry_ref[...]
    rolled = pltpu.roll(p, 1, axis=0)
    shifted = jnp.where(_iota(p.shape, 0) == 0, prev, rolled)
    carry_ref[...] = p[tb - 1:tb, :]
    p = p + (shifted - p) * mu_ref[...]

    r = p[:, 0:W]
    k = p[:, W:2 * W]
    v = p[:, 2 * W:3 * W]
    wa_l = p[:, 3 * W:3 * W + LANES]
    g_l = p[:, 3 * W + LANES:]

    lora_in = jnp.where(_iota(wa_l.shape, 1) < DECAY_LORA, jnp.tanh(wa_l), wa_l)
    lora = _bdot(lora_in, wwa_ref[...])
    w = -_softplus(-(w0_ref[...] + lora[:, :W])) - 0.5
    a = _sigmoid(a0_ref[...] + lora[:, W:])
    g = _bdot(_sigmoid(g_l), g2_ref[...])

    kk = k * kk_ref[...]
    ss = _dot_exact_rhs(kk * kk, e_ref[...])
    kk = kk / jnp.maximum(jnp.sqrt(ss), 1e-12)
    k = k * (1.0 + (a - 1.0) * ka_ref[...])

    r_out[0] = r
    k_out[0] = k
    v_out[0] = v
    an_out[0] = -kk
    b_out[0] = kk * a
    ld_out[0] = -jnp.exp(w)
    g_out[0] = g


def _rwkv_prep(p, mu, w0, a0, k_k, k_a, wwa, g2, e_head, tb=256):
    B, T, C = p.shape
    tb = min(tb, T)
    W = RWKV_W
    vec = lambda n: pl.BlockSpec((1, n), lambda b, i: (0, 0))
    full = lambda s: pl.BlockSpec(s, lambda b, i: (0,) * len(s))
    out_spec = pl.BlockSpec((1, tb, W), lambda b, i: (b, i, 0))
    return pl.pallas_call(
        _rwkv_prep_kernel,
        grid=(B, T // tb),
        in_specs=[pl.BlockSpec((1, tb, C), lambda b, i: (b, i, 0)),
                  vec(C), vec(W), vec(W), vec(W), vec(W),
                  full(wwa.shape), full(g2.shape), full(e_head.shape)],
        out_specs=[out_spec] * 7,
        out_shape=[jax.ShapeDtypeStruct((B, T, W), F32)] * 7,
        scratch_shapes=[pltpu.VMEM((1, C), F32)],
        compiler_params=_cparams(("arbitrary", "arbitrary")),
        name="rwkv_prep",
    )(p, mu, w0, a0, k_k, k_a, wwa, g2, e_head)


def _stack2(x):
    m0 = _iota(x.shape, 1) < HEAD_DIM
    return jnp.concatenate([jnp.where(m0, x, 0.0), jnp.where(m0, 0.0, x)], axis=0)


def _rwkv_scan_kernel(r_ref, k_ref, v_ref, an_ref, b_ref, ld_ref, g_ref,
                      lnw_ref, lnb_ref, rk_ref, e_ref, y_ref, state_ref):
    C = RWKV_CHUNK
    C2 = 2 * C
    nb = r_ref.shape[0]

    @pl.when(pl.program_id(0) == 0)
    def _():
        state_ref[...] = jnp.zeros_like(state_ref)

    row = _iota((C2, C2), 0)
    col = _iota((C2, C2), 1)
    strict = col < row
    incl = col <= row
    eye = (col == row).astype(F32)
    tri = (_iota((C, C), 1) <= _iota((C, C), 0)).astype(BF16)

    for bi in range(nb):
        y_pairs = []
        for pair in range(RWKV_HEADS // 2):
            sl = slice(pair * LANES, (pair + 1) * LANES)
            r = r_ref[bi, :, sl]
            k = k_ref[bi, :, sl]
            v = v_ref[bi, :, sl]
            an = an_ref[bi, :, sl]
            b = b_ref[bi, :, sl]
            ld = ld_ref[bi, :, sl]

            cum = _dot_exact_lhs(tri, ld)
            p_inc = jnp.exp(cum)
            p_prev = jnp.exp(cum - ld)
            p_inv = jnp.exp(-cum)
            p_end = p_inc[C - 1:C, :]

            rt = _stack2(r * p_inc)
            at = _stack2(an * p_prev)
            bt = b * p_inv
            kt = k * p_inv
            bts = _stack2(bt)
            kts = _stack2(kt)
            bhs = _stack2(bt * p_end)
            khs = _stack2(kt * p_end)
            vs = _stack2(v)

            lhs = jnp.concatenate([at, rt], axis=0)
            rhs = jnp.concatenate([bts, kts], axis=0)
            aa = _bdot_nt(lhs, rhs)
            n_ab = jnp.where(strict, aa[:C2, :C2], 0.0)
            a_ak = jnp.where(strict, aa[:C2, C2:], 0.0)
            a_rb = jnp.where(incl, aa[C2:, :C2], 0.0)
            a_rk = jnp.where(incl, aa[C2:, C2:], 0.0)

            inv = eye + n_ab
            pw = n_ab
            for _ in range(int(math.log2(C)) - 1):
                pw = _mm3(pw, pw)
                inv = inv + _mm3(pw, inv)

            w_t = _mm3(inv, at)
            u_t = _mm3(inv, _bdot(a_ak, vs))
            y_t = _bdot(a_rk, vs)

            idx = bi * (RWKV_HEADS // 2) + pair
            G = state_ref[idx]
            x = _bdot_nt(jnp.concatenate([w_t, rt], axis=0), G)
            u = x[:C2] + u_t
            y = x[C2:] + y_t + _bdot(a_rb, u)
            y_pairs.append(y[:C] + y[C:])

            uv = jnp.concatenate([u, vs], axis=0)
            bk = jnp.concatenate([bhs, khs], axis=0)
            state_ref[idx] = G * p_end + _bdot(uv.T, bk)

        y = jnp.concatenate(y_pairs, axis=-1)
        e = e_ref[...]
        inv_n = 1.0 / HEAD_DIM
        mean = _dot_exact_rhs(y, e) * inv_n
        yc = y - mean
        var = _dot_exact_rhs(yc * yc, e) * inv_n
        yn = yc * lax.rsqrt(var + RWKV_LN_EPS) * lnw_ref[...] + lnb_ref[...]
        r = r_ref[bi]
        bonus = _dot_exact_rhs(r * k_ref[bi] * rk_ref[...], e)
        y_ref[bi] = (yn + bonus * v_ref[bi]) * g_ref[bi]


def _rwkv_scan(r, k, v, an, b, ld, g, ln_w, ln_b, r_k, e_head):
    B, T, W = r.shape
    C = RWKV_CHUNK
    assert T % C == 0
    blk = pl.BlockSpec((B, C, W), lambda i: (0, i, 0))
    vec = pl.BlockSpec((1, W), lambda i: (0, 0))
    return pl.pallas_call(
        _rwkv_scan_kernel,
        grid=(T // C,),
        in_specs=[blk] * 7 + [vec, vec, vec, pl.BlockSpec(e_head.shape, lambda i: (0, 0))],
        out_specs=blk,
        out_shape=jax.ShapeDtypeStruct((B, T, W), F32),
        scratch_shapes=[pltpu.VMEM((B * RWKV_HEADS // 2, LANES, LANES), F32)],
        compiler_params=_cparams(("arbitrary",)),
        name="rwkv_scan",
    )(r, k, v, an, b, ld, g, ln_w, ln_b, r_k, e_head)


def _s5_kernel(u_ref, m1_ref, w1_ref, w2_ref, a1_ref, a2_ref, y_ref, *, chunks_per_seq):
    u = u_ref[0].astype(BF16)
    yi = jnp.dot(u, m1_ref[0], preferred_element_type=F32)
    x = jnp.dot(u, w1_ref[0], preferred_element_type=F32)
    crow = _iota(x.shape, 0) & (chunks_per_seq - 1)
    a1 = a1_ref[0]
    a2 = a2_ref[0]
    for s in range(int(math.log2(chunks_per_seq))):
        sh = 1 << s
        xs = jnp.where(crow >= sh, pltpu.roll(x, sh, axis=0), 0.0)
        x = x + a1[s:s + 1, :] * xs + a2[s:s + 1, :] * pltpu.roll(xs, S5_STATE, axis=1)
    s0 = jnp.where(crow >= 1, pltpu.roll(x, 1, axis=0), 0.0)
    y_ref[0] = yi + _bdot(s0, w2_ref[0])


def _s5_conv(u_t, m1, w1, w2, a1, a2, chunks_per_seq):
    G, R, K = u_t.shape
    assert chunks_per_seq & (chunks_per_seq - 1) == 0
    spec = lambda a: pl.BlockSpec((1,) + a.shape[1:], lambda g: (g, 0, 0))
    return pl.pallas_call(
        functools.partial(_s5_kernel, chunks_per_seq=chunks_per_seq),
        grid=(G,),
        in_specs=[spec(u_t), spec(m1), spec(w1), spec(w2), spec(a1), spec(a2)],
        out_specs=spec(u_t),
        out_shape=jax.ShapeDtypeStruct((G, R, K), F32),
        compiler_params=_cparams(("parallel",)),
        name="s5_conv",
    )(u_t, m1, w1, w2, a1, a2)


def _s5_weights(lam_re, lam_im, log_dt, b_re, b_im, c_re, c_im, chunks_per_seq):
    hp = lax.Precision.HIGHEST
    L = S5_CHUNK
    dt = jnp.exp(log_dt)[:, None]
    lr = lam_re * dt
    li = lam_im * dt
    mag = jnp.exp(lr)
    ab_re = mag * jnp.cos(li)
    ab_im = mag * jnp.sin(li)
    den = lam_re * lam_re + lam_im * lam_im
    nr = ab_re - 1.0
    coef_re = (nr * lam_re + ab_im * lam_im) / den
    coef_im = (ab_im * lam_re - nr * lam_im) / den
    bb_re = coef_re[..., None] * b_re - coef_im[..., None] * b_im
    bb_im = coef_re[..., None] * b_im + coef_im[..., None] * b_re

    def power(n):
        n = n.astype(F32)[:, None, None]
        m = jnp.exp(n * lr[None])
        return m * jnp.cos(n * li[None]), m * jnp.sin(n * li[None])

    pw_re, pw_im = power(jnp.arange(L + 1))
    cp_re = c_re[None] * pw_re[:, :, None, :] - c_im[None] * pw_im[:, :, None, :]
    cp_im = c_re[None] * pw_im[:, :, None, :] + c_im[None] * pw_re[:, :, None, :]
    kern = (jnp.einsum('nghp,gpk->nghk', cp_re[:L], bb_re, precision=hp)
            - jnp.einsum('nghp,gpk->nghk', cp_im[:L], bb_im, precision=hp))
    s_idx = jnp.arange(L)[:, None]
    t_idx = jnp.arange(L)[None, :]
    lag = t_idx - s_idx
    kt = kern[jnp.clip(lag, 0, L - 1)]
    kt = jnp.where((lag >= 0)[:, :, None, None, None], kt, 0.0)
    G = lam_re.shape[0]
    H = S5_GROUP_CH
    m1 = kt.transpose(2, 0, 4, 1, 3).reshape(G, L * H, L * H)
    qr = pw_re[L - 1 - jnp.arange(L)]
    qi = pw_im[L - 1 - jnp.arange(L)]
    w1_re = qr[..., None] * bb_re[None] - qi[..., None] * bb_im[None]
    w1_im = qr[..., None] * bb_im[None] + qi[..., None] * bb_re[None]
    w1 = jnp.concatenate([w1_re, w1_im], axis=2)
    w1 = w1.transpose(1, 0, 3, 2).reshape(G, L * H, 2 * S5_STATE)
    w2 = jnp.concatenate([cp_re[1:], -cp_im[1:]], axis=3)
    w2 = w2.transpose(1, 3, 0, 2).reshape(G, 2 * S5_STATE, L * H)
    nsteps = int(math.log2(chunks_per_seq))
    sr, si = power(L * (2 ** jnp.arange(nsteps)))
    a1 = jnp.concatenate([sr, sr], axis=2).transpose(1, 0, 2)
    a2 = jnp.concatenate([-si, si], axis=2).transpose(1, 0, 2)
    pad = (-nsteps) % 8
    a1 = jnp.pad(a1, ((0, 0), (0, pad), (0, 0)))
    a2 = jnp.pad(a2, ((0, 0), (0, pad), (0, 0)))
    return m1.astype(BF16), w1.astype(BF16), w2.astype(BF16), a1, a2


def _fox_prep_kernel(qk_ref, f_ref, qg_ref, kg_ref, bf_ref, q_out, k_out, carry_ref):
    @pl.when(pl.program_id(1) == 0)
    def _():
        carry_ref[...] = jnp.zeros_like(carry_ref)

    tb = f_ref.shape[1]
    x = f_ref[0] + bf_ref[...]
    logf = jnp.minimum(x, 0.0) - jnp.log(1.0 + jnp.exp(-jnp.abs(x)))
    tri = (_iota((tb, tb), 1) <= _iota((tb, tb), 0)).astype(BF16)
    cum = _dot_exact_lhs(tri, logf) + carry_ref[...]
    carry_ref[...] = cum[tb - 1:tb, :]
    c_hi = cum.astype(BF16).astype(F32)
    r1 = cum - c_hi
    c_mid = r1.astype(BF16).astype(F32)
    c_lo = (r1 - c_mid).astype(BF16).astype(F32)

    lane = _iota((tb, LANES), 1)
    D = HEAD_DIM
    scale = 1.0 / math.sqrt(D)
    for h in range(FOX_HEADS):
        hi = jnp.broadcast_to(c_hi[:, h:h + 1], (tb, LANES))
        mid = jnp.broadcast_to(c_mid[:, h:h + 1], (tb, LANES))
        lo = jnp.broadcast_to(c_lo[:, h:h + 1], (tb, LANES))
        q = qk_ref[0, :, h * LANES:(h + 1) * LANES]
        k = qk_ref[0, :, (FOX_HEADS + h) * LANES:(FOX_HEADS + h + 1) * LANES]
        qn = q * lax.rsqrt(jnp.sum(q * q, axis=-1, keepdims=True) * (1.0 / D) + NORM_EPS) * qg_ref[...] * scale
        kn = k * lax.rsqrt(jnp.sum(k * k, axis=-1, keepdims=True) * (1.0 / D) + NORM_EPS) * kg_ref[...]
        q_aug = jnp.where(lane < D, qn,
                          jnp.where(lane < D + 3, 1.0,
                                    jnp.where(lane == D + 3, hi,
                                              jnp.where(lane == D + 4, mid,
                                                        jnp.where(lane == D + 5, lo, 0.0)))))
        k_aug = jnp.where(lane < D, kn,
                          jnp.where(lane == D, -hi,
                                    jnp.where(lane == D + 1, -mid,
                                              jnp.where(lane == D + 2, -lo,
                                                        jnp.where(lane < D + FOX_AUG, 1.0, 0.0)))))
        q_out[0, :, h * LANES:(h + 1) * LANES] = q_aug.astype(BF16)
        k_out[0, :, h * LANES:(h + 1) * LANES] = k_aug.astype(BF16)


def _fox_prep(qk, f, q_g, k_g, b_f, tb=256):
    B, T, _ = qk.shape
    tb = min(tb, T)
    HW = FOX_HEADS * LANES
    vec = pl.BlockSpec((1, LANES), lambda b, i: (0, 0))
    out_spec = pl.BlockSpec((1, tb, HW), lambda b, i: (b, i, 0))
    return pl.pallas_call(
        _fox_prep_kernel,
        grid=(B, T // tb),
        in_specs=[pl.BlockSpec((1, tb, 2 * HW), lambda b, i: (b, i, 0)),
                  pl.BlockSpec((1, tb, LANES), lambda b, i: (b, i, 0)),
                  vec, vec, vec],
        out_specs=[out_spec, out_spec],
        out_shape=[jax.ShapeDtypeStruct((B, T, HW), BF16)] * 2,
        scratch_shapes=[pltpu.VMEM((1, LANES), F32)],
        compiler_params=_cparams(("arbitrary", "arbitrary")),
        name="fox_prep",
    )(qk, f, q_g, k_g, b_f)


def _fox_attn_kernel(q_ref, k_ref, v_ref, o_ref, *, tq):
    qi = pl.program_id(2)
    outs = []
    for hh in range(2):
        hs = slice(hh * LANES, (hh + 1) * LANES)
        q = q_ref[0, :, hs]

        def step(j, carry, masked):
            m, l, acc = carry
            start = pl.multiple_of(j * tq, tq)
            kb = k_ref[0, pl.ds(start, tq), hs]
            vb = v_ref[0, pl.ds(start, tq), :]
            s = lax.dot_general(q, kb, (((1,), (1,)), ((), ())), preferred_element_type=F32)
            if masked:
                s = jnp.where(_iota(s.shape, 1) <= _iota(s.shape, 0), s, NEG_BIG)
            m_new = jnp.maximum(m, jnp.max(s, axis=-1, keepdims=True))
            p = jnp.exp(s - m_new)
            alpha = jnp.exp(m - m_new)
            l = alpha * l + jnp.sum(p, axis=-1, keepdims=True)
            acc = alpha * acc + jnp.dot(p.astype(BF16), vb, preferred_element_type=F32)
            return m_new, l, acc

        init = (jnp.full((tq, 1), NEG_BIG, F32), jnp.zeros((tq, 1), F32), jnp.zeros((tq, LANES), F32))
        carry = lax.fori_loop(0, qi, lambda j, c: step(j, c, False), init)
        m, l, acc = step(qi, carry, True)
        outs.append(acc / l)
    o_ref[0] = jnp.where(_iota(outs[0].shape, 1) < HEAD_DIM, outs[0], outs[1])


def _fox_attn(q_aug, k_aug, v, tq=512):
    B, T, _ = q_aug.shape
    tq = min(tq, T)
    npairs = FOX_HEADS // 2
    return pl.pallas_call(
        functools.partial(_fox_attn_kernel, tq=tq),
        grid=(B, npairs, T // tq),
        in_specs=[pl.BlockSpec((1, tq, 2 * LANES), lambda b, h, i: (b, i, h)),
                  pl.BlockSpec((1, T, 2 * LANES), lambda b, h, i: (b, 0, h)),
                  pl.BlockSpec((1, T, LANES), lambda b, h, i: (b, 0, h))],
        out_specs=pl.BlockSpec((1, tq, LANES), lambda b, h, i: (b, i, h)),
        out_shape=jax.ShapeDtypeStruct((B, T, FOX_W), F32),
        compiler_params=_cparams(("parallel", "parallel", "arbitrary")),
        name="fox_attn",
    )(q_aug, k_aug, v)


def _gelu_tanh(x):
    return 0.5 * x * (1.0 + jnp.tanh(math.sqrt(2.0 / math.pi) * (x + 0.044715 * (x * x * x))))


def _mix_out_kernel(h_ref, yr_ref, ys_ref, us_ref, of_ref, og_ref, mk_ref, mv_ref,
                    d_ref, bglu_ref, sg_ref, fg_ref, xg_ref,
                    wglu_ref, wout_ref, wxq_ref, wxo_ref, o_ref):
    h = h_ref[...]
    ys = _gelu_tanh(ys_ref[...] + d_ref[...] * us_ref[...])
    ys = ys * _sigmoid(_bdot(ys, wglu_ref[...]) + bglu_ref[...])
    ys = _rms(ys, sg_ref[...])
    yf = _rms(of_ref[...] * _sigmoid(og_ref[...]), fg_ref[...])
    r0, r1 = RWKV_W, RWKV_W + S5_W
    h = h + (_bdot(yr_ref[...], wout_ref[0:r0, :]) + _bdot(ys, wout_ref[r0:r1, :])
             + _bdot(yf, wout_ref[r1:, :]))
    q = jnp.dot(_rms(h, xg_ref[...]).astype(BF16), wxq_ref[...], preferred_element_type=F32)
    scale = 1.0 / math.sqrt(XATTN_HEAD_DIM)
    upd = jnp.zeros_like(h)
    for hd in range(XATTN_HEADS):
        cs = slice(hd * XATTN_HEAD_DIM, (hd + 1) * XATTN_HEAD_DIM)
        s = _bdot_nt(q[:, cs], mk_ref[0, :, cs]) * scale
        s = s - jnp.max(s, axis=-1, keepdims=True)
        p = jnp.exp(s)
        p = p / jnp.sum(p, axis=-1, keepdims=True)
        o = _bdot(p, mv_ref[0, :, cs])
        upd = upd + _bdot(o, wxo_ref[cs, :])
    o_ref[...] = h + upd


def _mix_out(h, yr, ys, us, of, og, mem_k, mem_v, d, bglu, sg, fg, xg, wglu, wout, wxq, wxo,
             rows_per_batch, tm=256):
    n, dm = h.shape
    tm = min(tm, rows_per_batch)
    assert rows_per_batch % tm == 0
    bpb = rows_per_batch // tm
    rowblk = lambda w: pl.BlockSpec((tm, w), lambda i: (i, 0))
    vec = lambda w: pl.BlockSpec((1, w), lambda i: (0, 0))
    memblk = pl.BlockSpec((1,) + mem_k.shape[1:], lambda i: (i // bpb, 0, 0))
    return pl.pallas_call(
        _mix_out_kernel,
        grid=(n // tm,),
        in_specs=[rowblk(dm), rowblk(RWKV_W), rowblk(S5_W), rowblk(S5_W), rowblk(FOX_W), rowblk(FOX_W),
                  memblk, memblk,
                  vec(S5_W), vec(S5_W), vec(S5_W), vec(FOX_W), vec(dm),
                  _vmem_full(), _vmem_full(), _vmem_full(), _vmem_full()],
        out_specs=rowblk(dm),
        out_shape=jax.ShapeDtypeStruct((n, dm), F32),
        compiler_params=_cparams(("parallel",)),
        name="mix_out",
    )(h, yr, ys, us, of, og, mem_k, mem_v, d, bglu, sg, fg, xg, wglu, wout, wxq, wxo)


def _ffn_kernel(h_ref, g_ref, w1_ref, w2_ref, fg_ref, o_ref, *, ff_chunk, final_norm):
    h = h_ref[...]
    hn = _rms(h, g_ref[...]).astype(BF16)
    acc = h
    for c0 in range(0, w1_ref.shape[1], ff_chunk):
        z = jnp.dot(hn, w1_ref[:, c0:c0 + ff_chunk], preferred_element_type=F32)
        z = jnp.maximum(z, 0.0)
        acc = acc + jnp.dot((z * z).astype(BF16), w2_ref[c0:c0 + ff_chunk, :], preferred_element_type=F32)
    if final_norm:
        acc = _rms(acc, fg_ref[...])
    o_ref[...] = acc


def _ffn(h, g, w1, w2, fg, final_norm, tm=512):
    n, dm = h.shape
    tm = min(tm, n)
    assert n % tm == 0
    vec = pl.BlockSpec((1, dm), lambda i: (0, 0))
    return pl.pallas_call(
        functools.partial(_ffn_kernel, ff_chunk=1024, final_norm=final_norm),
        grid=(n // tm,),
        in_specs=[pl.BlockSpec((tm, dm), lambda i: (i, 0)), vec, _vmem_full(), _vmem_full(), vec],
        out_specs=pl.BlockSpec((tm, dm), lambda i: (i, 0)),
        out_shape=jax.ShapeDtypeStruct((n, dm), F32),
        compiler_params=_cparams(("parallel",)),
        name="ffn",
    )(h, g.reshape(1, dm), w1, w2, fg.reshape(1, dm))


def _head_block_ones(width):
    i = jnp.arange(width) // HEAD_DIM
    return (i[:, None] == i[None, :]).astype(BF16)


def _rwkv_col_order():
    W = RWKV_W
    r = list(range(0, W))
    wl = list(range(W, W + DECAY_LORA))
    k = list(range(W + DECAY_LORA, 2 * W + DECAY_LORA))
    v = list(range(2 * W + DECAY_LORA, 3 * W + DECAY_LORA))
    al = list(range(3 * W + DECAY_LORA, 3 * W + DECAY_LORA + AAA_LORA))
    gl = list(range(3 * W + DECAY_LORA + AAA_LORA, RWKV_IN))
    return jnp.array(r + k + v + wl + al + gl, jnp.int32)


def _pad_heads(w, nheads):
    d = w.shape[0]
    w = w.reshape(d, nheads, HEAD_DIM)
    w = jnp.pad(w, ((0, 0), (0, 0), (0, LANES - HEAD_DIM)))
    return w.reshape(d, nheads * LANES)


def _pad_lanes(v, width=LANES):
    return jnp.pad(v, (0, width - v.shape[0])).reshape(1, width)


_QK_W = 2 * FOX_HEADS * LANES
_COLS = {}
_c = 0
for _name, _w in (("rwkv", RWKV_IN), ("s5", S5_W), ("qk", _QK_W), ("v", FOX_W), ("og", FOX_W), ("f", LANES)):
    _COLS[_name] = (_c, _c + _w)
    _c += _w
_COL_ORDER = ("rwkv", "s5", "qk", "v", "og", "f")


def _layout_w_in(w_in):
    o = RWKV_IN + S5_W
    w_r = w_in[:, :RWKV_IN][:, _rwkv_col_order()]
    w_s = w_in[:, RWKV_IN:o]
    w_q = _pad_heads(w_in[:, o:o + FOX_W], FOX_HEADS)
    w_k = _pad_heads(w_in[:, o + FOX_W:o + 2 * FOX_W], FOX_HEADS)
    w_v = w_in[:, o + 2 * FOX_W:o + 3 * FOX_W]
    w_g = w_in[:, o + 3 * FOX_W:o + 4 * FOX_W]
    w_f = jnp.pad(w_in[:, o + 4 * FOX_W:], ((0, 0), (0, LANES - FOX_HEADS)))
    return jnp.concatenate([w_r, w_s, w_q, w_k, w_v, w_g, w_f], axis=1).astype(BF16)


def kernel(x, mem, mem_norm_g, w_mem_kv, mix_norm_g, w_in, rwkv_mu, rwkv_w0, rwkv_w2, rwkv_a0, rwkv_a2, rwkv_g2, rwkv_k_k, rwkv_k_a, rwkv_r_k, rwkv_ln_w, rwkv_ln_b, s5_lam_re, s5_lam_im, s5_log_dt, s5_b_re, s5_b_im, s5_c_re, s5_c_im, s5_d, s5_w_glu, s5_b_glu, s5_out_g, fox_q_g, fox_k_g, fox_b_f, fox_out_g, w_out, xattn_norm_g, w_xq, w_xo, ffn_norm_g, w_ffn1, w_ffn2, final_norm_g):
    B, T, D = x.shape
    M = mem.shape[1]
    N = B * T
    depth = w_in.shape[0]
    W = RWKV_W
    nchunks = T // S5_CHUNK
    e_head = _head_block_ones(W)

    mem_kv = _norm_matmul(mem.reshape(B * M, D), mem_norm_g, w_mem_kv.astype(BF16),
                          [(0, D), (D, 2 * D)], [BF16, BF16], tm=256)
    mem_k = mem_kv[0].reshape(B, M, D)
    mem_v = mem_kv[1].reshape(B, M, D)

    col_ranges = [_COLS[n] for n in _COL_ORDER]
    col_dtypes = [F32, F32, F32, BF16, F32, F32]
    rorder = _rwkv_col_order()

    h = x.reshape(N, D)
    for l in range(depth):
        p_r, p_s, p_qk, p_v, p_og, p_f = _norm_matmul(
            h, mix_norm_g[l], _layout_w_in(w_in[l]), col_ranges, col_dtypes)

        wwa = jnp.zeros((LANES, 2 * W), F32)
        wwa = wwa.at[:DECAY_LORA, :W].set(rwkv_w2[l]).at[DECAY_LORA:, W:].set(rwkv_a2[l]).astype(BF16)
        row = lambda v: v.reshape(1, -1)
        r, k, v, an, b, ld, g = _rwkv_prep(
            p_r.reshape(B, T, RWKV_IN), row(rwkv_mu[l][rorder]), row(rwkv_w0[l]), row(rwkv_a0[l]),
            row(rwkv_k_k[l]), row(rwkv_k_a[l]), wwa, rwkv_g2[l].astype(BF16), e_head)
        y_r = _rwkv_scan(r, k, v, an, b, ld, g, row(rwkv_ln_w[l]), row(rwkv_ln_b[l]),
                         row(rwkv_r_k[l]), e_head).reshape(N, W)

        m1, w1, w2, a1, a2 = _s5_weights(s5_lam_re[l], s5_lam_im[l], s5_log_dt[l], s5_b_re[l], s5_b_im[l],
                                         s5_c_re[l], s5_c_im[l], nchunks)
        u_t = p_s.reshape(B * nchunks, S5_CHUNK, S5_GROUPS, S5_GROUP_CH).transpose(2, 0, 1, 3)
        u_t = u_t.reshape(S5_GROUPS, B * nchunks, S5_CHUNK * S5_GROUP_CH)
        y_t = _s5_conv(u_t, m1, w1, w2, a1, a2, nchunks)
        y_s = y_t.reshape(S5_GROUPS, B * nchunks, S5_CHUNK, S5_GROUP_CH).transpose(1, 2, 0, 3).reshape(N, S5_W)

        q_aug, k_aug = _fox_prep(p_qk.reshape(B, T, _QK_W), p_f.reshape(B, T, LANES),
                                 _pad_lanes(fox_q_g[l]), _pad_lanes(fox_k_g[l]), _pad_lanes(fox_b_f[l]))
        o_f = _fox_attn(q_aug, k_aug, p_v.reshape(B, T, FOX_W)).reshape(N, FOX_W)

        h = _mix_out(h, y_r, y_s, p_s, o_f, p_og, mem_k, mem_v,
                     row(s5_d[l]), row(s5_b_glu[l]), row(s5_out_g[l]), row(fox_out_g[l]), row(xattn_norm_g[l]),
                     s5_w_glu[l].astype(BF16), w_out[l].astype(BF16), w_xq[l].astype(BF16),
                     w_xo[l].astype(BF16), rows_per_batch=T)
        h = _ffn(h, ffn_norm_g[l], w_ffn1[l].astype(BF16), w_ffn2[l].astype(BF16), final_norm_g,
                 final_norm=(l == depth - 1))
    return h.reshape(B, T, D)
```

```python
import functools
import math

import jax
import jax.numpy as jnp
from jax import lax
from jax.experimental import pallas as pl
from jax.experimental.pallas import tpu as pltpu

F32 = jnp.float32
BF16 = jnp.bfloat16

D_MODEL = 1024
RWKV_HEADS = 4
HEAD_DIM = 64
RWKV_W = RWKV_HEADS * HEAD_DIM
DECAY_LORA = 64
AAA_LORA = 64
GATE_LORA = 128
RWKV_IN = 3 * RWKV_W + DECAY_LORA + AAA_LORA + GATE_LORA
S5_GROUPS = 16
S5_GROUP_CH = 16
S5_W = S5_GROUPS * S5_GROUP_CH
S5_STATE = 64
FOX_HEADS = 8
FOX_W = FOX_HEADS * HEAD_DIM
XATTN_HEADS = 4
XATTN_HEAD_DIM = D_MODEL // XATTN_HEADS
NORM_EPS = 1e-6
RWKV_LN_EPS = 64e-5

LANES = 128
RWKV_CHUNK = 64
S5_CHUNK = 16
FOX_AUG = 6
FOX_BLOCK = 512
VMEM_LIMIT = 56 * 1024 * 1024

NEG_BIG = -1e30
LOG2E = 1.4426950408889634


def _cparams(sem):
    return pltpu.CompilerParams(dimension_semantics=sem, vmem_limit_bytes=VMEM_LIMIT)


def _bdot(a, b):
    return jnp.dot(a.astype(BF16), b.astype(BF16), preferred_element_type=F32)


def _bdot_nt(a, b):
    return lax.dot_general(a.astype(BF16), b.astype(BF16), (((1,), (1,)), ((), ())),
                           preferred_element_type=F32)


def _split2(x):
    hi = x.astype(BF16)
    lo = (x - hi.astype(F32)).astype(BF16)
    return hi, lo


def _split3(x):
    hi = x.astype(BF16)
    r1 = x - hi.astype(F32)
    mid = r1.astype(BF16)
    lo = (r1 - mid.astype(F32)).astype(BF16)
    return hi, mid, lo


def _dot_exact_lhs(a_bf16, x):
    hi, mid, lo = _split3(x)
    d = lambda p: jnp.dot(a_bf16, p, preferred_element_type=F32)
    return d(hi) + d(mid) + d(lo)


def _dot_exact_rhs(x, b_bf16):
    hi, lo = _split2(x)
    d = lambda p: jnp.dot(p, b_bf16, preferred_element_type=F32)
    return d(hi) + d(lo)


def _mm3(a, b):
    ah, al = _split2(a)
    bh, bl = _split2(b)
    d = lambda p, q: jnp.dot(p, q, preferred_element_type=F32)
    return d(ah, bh) + d(ah, bl) + d(al, bh)


_mm_inv = _mm3


def _sigmoid(x):
    return 1.0 / (1.0 + jnp.exp(-x))


def _softplus(x):
    return jnp.maximum(x, 0.0) + jnp.log(1.0 + jnp.exp(-jnp.abs(x)))


def _rms(x, g, eps=NORM_EPS):
    return x * lax.rsqrt(jnp.mean(x * x, axis=-1, keepdims=True) + eps) * g


def _iota(shape, dim):
    return lax.broadcasted_iota(jnp.int32, shape, dim)


def _vmem_full():
    return pl.BlockSpec(memory_space=pltpu.VMEM)


def _norm_matmul_kernel(x_ref, g_ref, w_ref, *out_refs, col_ranges, col_chunk):
    xn = _rms(x_ref[...], g_ref[...]).astype(BF16)
    for o_ref, (c0, c1) in zip(out_refs, col_ranges):
        for cc in range(c0, c1, col_chunk):
            ce = min(cc + col_chunk, c1)
            o_ref[:, cc - c0:ce - c0] = jnp.dot(
                xn, w_ref[:, cc:ce], preferred_element_type=F32).astype(o_ref.dtype)


def _norm_matmul(x, g, w, col_ranges, out_dtypes, tm=512):
    n, d = x.shape
    tm = min(tm, n)
    assert n % tm == 0
    kern = functools.partial(_norm_matmul_kernel, col_ranges=tuple(col_ranges), col_chunk=512)
    return pl.pallas_call(
        kern,
        grid=(n // tm,),
        in_specs=[pl.BlockSpec((tm, d), lambda i: (i, 0)),
                  pl.BlockSpec((1, d), lambda i: (0, 0)),
                  _vmem_full()],
        out_specs=[pl.BlockSpec((tm, c1 - c0), lambda i: (i, 0)) for (c0, c1) in col_ranges],
        out_shape=[jax.ShapeDtypeStruct((n, c1 - c0), dt) for (c0, c1), dt in zip(col_ranges, out_dtypes)],
        compiler_params=_cparams(("parallel",)),
        name="norm_matmul",
    )(x, g.reshape(1, d), w)


def _rwkv_prep_kernel(p_ref, mu_ref, w0_ref, a0_ref, kk_ref, ka_ref, wwa_ref, g2_ref, e_ref,
                      r_out, k_out, v_out, an_out, b_out, ld_out, g_out, carry_ref):
    W = RWKV_W

    @pl.when(pl.program_id(1) == 0)
    def _():
        carry_ref[...] = jnp.zeros_like(carry_ref)

    p = p_ref[0]
    tb = p.shape[0]
    prev = carry_ref[...]
    rolled = pltpu.roll(p, 1, axis=0)
    shifted = jnp.where(_iota(p.shape, 0) == 0, prev, rolled)
    carry_ref[...] = p[tb - 1:tb, :]
    p = p + (shifted - p) * mu_ref[...]

    r = p[:, 0:W]
    k = p[:, W:2 * W]
    v = p[:, 2 * W:3 * W]
    wa_l = p[:, 3 * W:3 * W + LANES]
    g_l = p[:, 3 * W + LANES:]

    lora_in = jnp.where(_iota(wa_l.shape, 1) < DECAY_LORA, jnp.tanh(wa_l), wa_l)
    lora = _bdot(lora_in, wwa_ref[...])
    w = -_softplus(-(w0_ref[...] + lora[:, :W])) - 0.5
    a = _sigmoid(a0_ref[...] + lora[:, W:])
    g = _bdot(_sigmoid(g_l), g2_ref[...])

    kk = k * kk_ref[...]
    ss = _dot_exact_rhs(kk * kk, e_ref[...])
    kk = kk / jnp.maximum(jnp.sqrt(ss), 1e-12)
    k = k * (1.0 + (a - 1.0) * ka_ref[...])

    r_out[0] = r
    k_out[0] = k
    v_out[0] = v
    an_out[0] = -kk
    b_out[0] = kk * a
    ld_out[0] = -jnp.exp(w)
    g_out[0] = g


def _rwkv_prep(p, mu, w0, a0, k_k, k_a, wwa, g2, e_head, tb=256):
    B, T, C = p.shape
    tb = min(tb, T)
    W = RWKV_W
    vec = lambda n: pl.BlockSpec((1, n), lambda b, i: (0, 0))
    full = lambda s: pl.BlockSpec(s, lambda b, i: (0,) * len(s))
    out_spec = pl.BlockSpec((1, tb, W), lambda b, i: (b, i, 0))
    return pl.pallas_call(
        _rwkv_prep_kernel,
        grid=(B, T // tb),
        in_specs=[pl.BlockSpec((1, tb, C), lambda b, i: (b, i, 0)),
                  vec(C), vec(W), vec(W), vec(W), vec(W),
                  full(wwa.shape), full(g2.shape), full(e_head.shape)],
        out_specs=[out_spec] * 7,
        out_shape=[jax.ShapeDtypeStruct((B, T, W), F32)] * 7,
        scratch_shapes=[pltpu.VMEM((1, C), F32)],
        compiler_params=_cparams(("arbitrary", "arbitrary")),
        name="rwkv_prep",
    )(p, mu, w0, a0, k_k, k_a, wwa, g2, e_head)


def _stack_heads(x, head_mask):
    return jnp.where(head_mask, jnp.concatenate([x] * RWKV_HEADS, axis=0), 0.0)


def _rwkv_scan_kernel(r_ref, k_ref, v_ref, an_ref, b_ref, ld_ref, g_ref,
                      lnw_ref, lnb_ref, rk_ref, y_ref, state_ref):
    C = RWKV_CHUNK
    W = RWKV_W
    nb = r_ref.shape[0]

    @pl.when(pl.program_id(0) == 0)
    def _():
        state_ref[...] = jnp.zeros_like(state_ref)

    row = _iota((W, W), 0)
    col = _iota((W, W), 1)
    strict = col < row
    incl = col <= row
    eye = (col == row).astype(F32)
    head_mask = (row // C) == (col // HEAD_DIM)
    tri = (_iota((C, C), 1) <= _iota((C, C), 0)).astype(BF16)
    stack = lambda z: _stack_heads(z, head_mask)

    for bi in range(nb):
        r = r_ref[bi]
        k = k_ref[bi]
        v = v_ref[bi]
        ld = ld_ref[bi]

        cum = _dot_exact_lhs(tri, ld)
        p_inc = jnp.exp(cum)
        p_prev = jnp.exp(cum - ld)
        p_inv = jnp.exp(-cum)
        p_end = p_inc[C - 1:C, :]

        rt = stack(r * p_inc)
        at = stack(an_ref[bi] * p_prev)
        bt = b_ref[bi] * p_inv
        kt = k * p_inv
        vs = stack(v)
        bk = jnp.concatenate([stack(bt * p_end), stack(kt * p_end)], axis=0)

        lhs = jnp.concatenate([at, rt], axis=0)
        rhs = jnp.concatenate([stack(bt), stack(kt)], axis=0)
        aa = _bdot_nt(lhs, rhs)
        n_ab = jnp.where(strict, aa[:W, :W], 0.0)
        a_ak = jnp.where(strict, aa[:W, W:], 0.0)
        a_rb = jnp.where(incl, aa[W:, :W], 0.0)
        a_rk = jnp.where(incl, aa[W:, W:], 0.0)

        inv = eye + n_ab
        pw = n_ab
        for _ in range(int(math.log2(C)) - 1):
            pw = _mm_inv(pw, pw)
            inv = inv + _mm_inv(pw, inv)

        w_t = _mm_inv(inv, at)
        u_t = _mm_inv(inv, _bdot(a_ak, vs))
        y_t = _bdot(a_rk, vs)

        H = state_ref[bi]
        x = _bdot(jnp.concatenate([w_t, rt], axis=0), H)
        u = x[:W] + u_t
        y = x[W:] + y_t + _bdot(a_rb, u)

        uv = jnp.concatenate([u, vs], axis=0)
        p_col = jnp.broadcast_to(p_end, (8, W)).T[:, 0:1]
        state_ref[bi] = H * p_col + _bdot(bk.T, uv)

        inv_n = 1.0 / HEAD_DIM
        mean = jnp.sum(y, axis=-1, keepdims=True) * inv_n
        yc = jnp.where(head_mask, y - mean, 0.0)
        var = jnp.sum(yc * yc, axis=-1, keepdims=True) * inv_n
        yn = jnp.where(head_mask, yc * lax.rsqrt(var + RWKV_LN_EPS) * lnw_ref[...] + lnb_ref[...], 0.0)
        bonus = jnp.sum(stack(r * k * rk_ref[...]), axis=-1, keepdims=True)
        tot = yn + bonus * vs
        out = tot[0:C]
        for h in range(1, RWKV_HEADS):
            out = out + tot[h * C:(h + 1) * C]
        y_ref[bi] = out * g_ref[bi]


def _rwkv_scan(r, k, v, an, b, ld, g, ln_w, ln_b, r_k):
    B, T, W = r.shape
    C = RWKV_CHUNK
    assert T % C == 0 and RWKV_HEADS * C == W
    blk = pl.BlockSpec((B, C, W), lambda i: (0, i, 0))
    vec = pl.BlockSpec((1, W), lambda i: (0, 0))
    return pl.pallas_call(
        _rwkv_scan_kernel,
        grid=(T // C,),
        in_specs=[blk] * 7 + [vec, vec, vec],
        out_specs=blk,
        out_shape=jax.ShapeDtypeStruct((B, T, W), F32),
        scratch_shapes=[pltpu.VMEM((B, W, W), F32)],
        compiler_params=_cparams(("arbitrary",)),
        name="rwkv_scan",
    )(r, k, v, an, b, ld, g, ln_w, ln_b, r_k)


def _s5_kernel(u_ref, m1_ref, w1_ref, w2_ref, a1_ref, a2_ref, y_ref, *, chunks_per_seq):
    u = u_ref[0].astype(BF16)
    yi = jnp.dot(u, m1_ref[0], preferred_element_type=F32)
    x = jnp.dot(u, w1_ref[0], preferred_element_type=F32)
    crow = _iota(x.shape, 0) & (chunks_per_seq - 1)
    a1 = a1_ref[0]
    a2 = a2_ref[0]
    for s in range(int(math.log2(chunks_per_seq))):
        sh = 1 << s
        xs = jnp.where(crow >= sh, pltpu.roll(x, sh, axis=0), 0.0)
        x = x + a1[s:s + 1, :] * xs + a2[s:s + 1, :] * pltpu.roll(xs, S5_STATE, axis=1)
    s0 = jnp.where(crow >= 1, pltpu.roll(x, 1, axis=0), 0.0)
    y_ref[0] = yi + _bdot(s0, w2_ref[0])


def _s5_conv(u_t, m1, w1, w2, a1, a2, chunks_per_seq):
    G, R, K = u_t.shape
    assert chunks_per_seq & (chunks_per_seq - 1) == 0
    spec = lambda a: pl.BlockSpec((1,) + a.shape[1:], lambda g: (g, 0, 0))
    return pl.pallas_call(
        functools.partial(_s5_kernel, chunks_per_seq=chunks_per_seq),
        grid=(G,),
        in_specs=[spec(u_t), spec(m1), spec(w1), spec(w2), spec(a1), spec(a2)],
        out_specs=spec(u_t),
        out_shape=jax.ShapeDtypeStruct((G, R, K), F32),
        compiler_params=_cparams(("parallel",)),
        name="s5_conv",
    )(u_t, m1, w1, w2, a1, a2)


def _s5_weights(lam_re, lam_im, log_dt, b_re, b_im, c_re, c_im, chunks_per_seq):
    hp = lax.Precision.HIGHEST
    L = S5_CHUNK
    dt = jnp.exp(log_dt)[:, None]
    lr = lam_re * dt
    li = lam_im * dt
    mag = jnp.exp(lr)
    ab_re = mag * jnp.cos(li)
    ab_im = mag * jnp.sin(li)
    den = lam_re * lam_re + lam_im * lam_im
    nr = ab_re - 1.0
    coef_re = (nr * lam_re + ab_im * lam_im) / den
    coef_im = (ab_im * lam_re - nr * lam_im) / den
    bb_re = coef_re[..., None] * b_re - coef_im[..., None] * b_im
    bb_im = coef_re[..., None] * b_im + coef_im[..., None] * b_re

    def power(n):
        n = n.astype(F32)[:, None, None]
        m = jnp.exp(n * lr[None])
        return m * jnp.cos(n * li[None]), m * jnp.sin(n * li[None])

    pw_re, pw_im = power(jnp.arange(L + 1))
    cp_re = c_re[None] * pw_re[:, :, None, :] - c_im[None] * pw_im[:, :, None, :]
    cp_im = c_re[None] * pw_im[:, :, None, :] + c_im[None] * pw_re[:, :, None, :]
    kern = (jnp.einsum('nghp,gpk->nghk', cp_re[:L], bb_re, precision=hp)
            - jnp.einsum('nghp,gpk->nghk', cp_im[:L], bb_im, precision=hp))
    s_idx = jnp.arange(L)[:, None]
    t_idx = jnp.arange(L)[None, :]
    lag = t_idx - s_idx
    kt = kern[jnp.clip(lag, 0, L - 1)]
    kt = jnp.where((lag >= 0)[:, :, None, None, None], kt, 0.0)
    G = lam_re.shape[0]
    H = S5_GROUP_CH
    m1 = kt.transpose(2, 0, 4, 1, 3).reshape(G, L * H, L * H)
    qr = pw_re[L - 1 - jnp.arange(L)]
    qi = pw_im[L - 1 - jnp.arange(L)]
    w1_re = qr[..., None] * bb_re[None] - qi[..., None] * bb_im[None]
    w1_im = qr[..., None] * bb_im[None] + qi[..., None] * bb_re[None]
    w1 = jnp.concatenate([w1_re, w1_im], axis=2)
    w1 = w1.transpose(1, 0, 3, 2).reshape(G, L * H, 2 * S5_STATE)
    w2 = jnp.concatenate([cp_re[1:], -cp_im[1:]], axis=3)
    w2 = w2.transpose(1, 3, 0, 2).reshape(G, 2 * S5_STATE, L * H)
    nsteps = int(math.log2(chunks_per_seq))
    sr, si = power(L * (2 ** jnp.arange(nsteps)))
    a1 = jnp.concatenate([sr, sr], axis=2).transpose(1, 0, 2)
    a2 = jnp.concatenate([-si, si], axis=2).transpose(1, 0, 2)
    pad = (-nsteps) % 8
    a1 = jnp.pad(a1, ((0, 0), (0, pad), (0, 0)))
    a2 = jnp.pad(a2, ((0, 0), (0, pad), (0, 0)))
    return m1.astype(BF16), w1.astype(BF16), w2.astype(BF16), a1, a2


def _fox_prep_kernel(qk_ref, v_ref, f_ref, qg_ref, kg_ref, bf_ref, q_out, k_out, vt_out, carry_ref):
    @pl.when(pl.program_id(1) == 0)
    def _():
        carry_ref[...] = jnp.zeros_like(carry_ref)

    tb = f_ref.shape[1]
    vt = v_ref[0].T
    ones_row = (_iota((LANES - HEAD_DIM, tb), 0) == 0).astype(F32)
    for h in range(FOX_HEADS):
        vt_out[0, h, 0] = jnp.concatenate(
            [vt[h * HEAD_DIM:(h + 1) * HEAD_DIM, :], ones_row], axis=0).astype(BF16)

    x = f_ref[0] + bf_ref[...]
    logf = jnp.minimum(x, 0.0) - jnp.log(1.0 + jnp.exp(-jnp.abs(x)))
    tri = (_iota((tb, tb), 1) <= _iota((tb, tb), 0)).astype(BF16)
    cum = _dot_exact_lhs(tri, logf) + carry_ref[...]
    carry_ref[...] = cum[tb - 1:tb, :]
    c2 = cum * LOG2E
    c_hi = c2.astype(BF16).astype(F32)
    r1 = c2 - c_hi
    c_mid = r1.astype(BF16).astype(F32)
    c_lo = (r1 - c_mid).astype(BF16).astype(F32)

    lane = _iota((tb, LANES), 1)
    D = HEAD_DIM
    scale = LOG2E / math.sqrt(D)
    for h in range(FOX_HEADS):
        hi = jnp.broadcast_to(c_hi[:, h:h + 1], (tb, LANES))
        mid = jnp.broadcast_to(c_mid[:, h:h + 1], (tb, LANES))
        lo = jnp.broadcast_to(c_lo[:, h:h + 1], (tb, LANES))
        q = qk_ref[0, :, h * LANES:(h + 1) * LANES]
        k = qk_ref[0, :, (FOX_HEADS + h) * LANES:(FOX_HEADS + h + 1) * LANES]
        qn = q * lax.rsqrt(jnp.sum(q * q, axis=-1, keepdims=True) * (1.0 / D) + NORM_EPS) * qg_ref[...] * scale
        kn = k * lax.rsqrt(jnp.sum(k * k, axis=-1, keepdims=True) * (1.0 / D) + NORM_EPS) * kg_ref[...]
        q_aug = jnp.where(lane < D, qn,
                          jnp.where(lane < D + 3, 1.0,
                                    jnp.where(lane == D + 3, hi,
                                              jnp.where(lane == D + 4, mid,
                                                        jnp.where(lane == D + 5, lo, 0.0)))))
        k_aug = jnp.where(lane < D, kn,
                          jnp.where(lane == D, -hi,
                                    jnp.where(lane == D + 1, -mid,
                                              jnp.where(lane == D + 2, -lo,
                                                        jnp.where(lane < D + FOX_AUG, 1.0, 0.0)))))
        q_out[0, :, h * LANES:(h + 1) * LANES] = q_aug.astype(BF16)
        k_out[0, :, h * LANES:(h + 1) * LANES] = k_aug.astype(BF16)


def _fox_prep(qk, v, f, q_g, k_g, b_f, tb):
    B, T, _ = qk.shape
    HW = FOX_HEADS * LANES
    vec = pl.BlockSpec((1, LANES), lambda b, i: (0, 0))
    out_spec = pl.BlockSpec((1, tb, HW), lambda b, i: (b, i, 0))
    return pl.pallas_call(
        _fox_prep_kernel,
        grid=(B, T // tb),
        in_specs=[pl.BlockSpec((1, tb, 2 * HW), lambda b, i: (b, i, 0)),
                  pl.BlockSpec((1, tb, FOX_W), lambda b, i: (b, i, 0)),
                  pl.BlockSpec((1, tb, LANES), lambda b, i: (b, i, 0)),
                  vec, vec, vec],
        out_specs=[out_spec, out_spec,
                   pl.BlockSpec((1, FOX_HEADS, 1, LANES, tb), lambda b, i: (b, 0, i, 0, 0))],
        out_shape=[jax.ShapeDtypeStruct((B, T, HW), BF16)] * 2
        + [jax.ShapeDtypeStruct((B, FOX_HEADS, T // tb, LANES, tb), BF16)],
        scratch_shapes=[pltpu.VMEM((1, LANES), F32)],
        compiler_params=_cparams(("arbitrary", "arbitrary")),
        name="fox_prep",
    )(qk, v, f, q_g, k_g, b_f)


def _fox_attn_kernel(q_ref, k_ref, vt_ref, o_ref, m_ref, acc_ref, sa_ref, sb_ref, *, tq):
    qi = pl.program_id(2)
    m_ref[...] = jnp.full(m_ref.shape, NEG_BIG, F32)
    acc_ref[...] = jnp.zeros_like(acc_ref)

    tqh = tq // 2

    chains = [(hh, half) for hh in range(2) for half in range(2)]

    def scores(j, s_ref):
        start = pl.multiple_of(j * tq, tq)
        for c, (hh, half) in enumerate(chains):
            hs = slice(hh * LANES, (hh + 1) * LANES)
            qs = slice(half * tqh, (half + 1) * tqh)
            s_ref[c] = lax.dot_general(k_ref[0, pl.ds(start, tq), hs], q_ref[0, qs, hs],
                                       (((1,), (1,)), ((), ())),
                                       preferred_element_type=F32)

    def softmax_pv(j, s_ref, masked):
        for c, (hh, half) in enumerate(chains):
            st = s_ref[c]
            qs = slice(half * tqh, (half + 1) * tqh)
            if masked:
                keep = _iota(st.shape, 0) <= _iota(st.shape, 1) + half * tqh
                st = jnp.where(keep, st, NEG_BIG)
            m_old = m_ref[hh, :, qs]
            m_new = jnp.maximum(m_old, jnp.max(st, axis=0, keepdims=True))
            p = jnp.exp2(st - m_new).astype(BF16)
            alpha = jnp.exp2(m_old - m_new)
            m_ref[hh, :, qs] = m_new
            acc_ref[hh, :, qs] = alpha * acc_ref[hh, :, qs] + jnp.dot(vt_ref[0, hh, j], p,
                                                                       preferred_element_type=F32)

    def body(i, carry):
        j = 2 * i
        scores(j + 1, sb_ref)
        softmax_pv(j, sa_ref, False)
        scores(j + 2, sa_ref)
        softmax_pv(j + 1, sb_ref, False)
        return carry

    scores(0, sa_ref)
    lax.fori_loop(0, qi // 2, body, 0)

    @pl.when(qi % 2 == 0)
    def _():
        softmax_pv(qi, sa_ref, True)

    @pl.when(qi % 2 == 1)
    def _():
        scores(qi, sb_ref)
        softmax_pv(qi - 1, sa_ref, False)
        softmax_pv(qi, sb_ref, True)

    outs = []
    for hh in range(2):
        acc = acc_ref[hh]
        outs.append(acc[:HEAD_DIM, :] / acc[HEAD_DIM:HEAD_DIM + 1, :])
    o_ref[0] = jnp.concatenate(outs, axis=0).T


def _fox_attn(q_aug, k_aug, vt, tq):
    B, T, _ = q_aug.shape
    npairs = FOX_HEADS // 2
    return pl.pallas_call(
        functools.partial(_fox_attn_kernel, tq=tq),
        grid=(B, npairs, T // tq),
        in_specs=[pl.BlockSpec((1, tq, 2 * LANES), lambda b, h, i: (b, i, h)),
                  pl.BlockSpec((1, T, 2 * LANES), lambda b, h, i: (b, 0, h)),
                  pl.BlockSpec((1, 2, T // tq, LANES, tq), lambda b, h, i: (b, h, 0, 0, 0))],
        out_specs=pl.BlockSpec((1, tq, LANES), lambda b, h, i: (b, i, h)),
        out_shape=jax.ShapeDtypeStruct((B, T, FOX_W), F32),
        scratch_shapes=[pltpu.VMEM((2, 1, tq), F32), pltpu.VMEM((2, LANES, tq), F32),
                        pltpu.VMEM((4, tq, tq // 2), F32), pltpu.VMEM((4, tq, tq // 2), F32)],
        compiler_params=_cparams(("parallel", "parallel", "arbitrary")),
        name="fox_attn",
    )(q_aug, k_aug, vt)


def _gelu_tanh(x):
    return 0.5 * x * (1.0 + jnp.tanh(math.sqrt(2.0 / math.pi) * (x + 0.044715 * (x * x * x))))


def _mix_out_kernel(h_ref, yr_ref, ys_ref, us_ref, of_ref, og_ref, mk_ref, mv_ref,
                    d_ref, bglu_ref, sg_ref, fg_ref, xg_ref,
                    wglu_ref, wout_ref, wxq_ref, wxo_ref, o_ref):
    h = h_ref[...]
    ys = _gelu_tanh(ys_ref[...] + d_ref[...] * us_ref[...])
    ys = ys * _sigmoid(_bdot(ys, wglu_ref[...]) + bglu_ref[...])
    ys = _rms(ys, sg_ref[...])
    yf = _rms(of_ref[...] * _sigmoid(og_ref[...]), fg_ref[...])
    r0, r1 = RWKV_W, RWKV_W + S5_W
    h = h + (_bdot(yr_ref[...], wout_ref[0:r0, :]) + _bdot(ys, wout_ref[r0:r1, :])
             + _bdot(yf, wout_ref[r1:, :]))
    q = jnp.dot(_rms(h, xg_ref[...]).astype(BF16), wxq_ref[...], preferred_element_type=F32)
    scale = 1.0 / math.sqrt(XATTN_HEAD_DIM)
    upd = jnp.zeros_like(h)
    for hd in range(XATTN_HEADS):
        cs = slice(hd * XATTN_HEAD_DIM, (hd + 1) * XATTN_HEAD_DIM)
        s = _bdot_nt(q[:, cs], mk_ref[0, :, cs]) * scale
        s = s - jnp.max(s, axis=-1, keepdims=True)
        p = jnp.exp(s)
        p = p / jnp.sum(p, axis=-1, keepdims=True)
        o = _bdot(p, mv_ref[0, :, cs])
        upd = upd + _bdot(o, wxo_ref[cs, :])
    o_ref[...] = h + upd


def _mix_out(h, yr, ys, us, of, og, mem_k, mem_v, d, bglu, sg, fg, xg, wglu, wout, wxq, wxo,
             rows_per_batch, tm=256):
    n, dm = h.shape
    tm = min(tm, rows_per_batch)
    assert rows_per_batch % tm == 0
    bpb = rows_per_batch // tm
    rowblk = lambda w: pl.BlockSpec((tm, w), lambda i: (i, 0))
    vec = lambda w: pl.BlockSpec((1, w), lambda i: (0, 0))
    memblk = pl.BlockSpec((1,) + mem_k.shape[1:], lambda i: (i // bpb, 0, 0))
    return pl.pallas_call(
        _mix_out_kernel,
        grid=(n // tm,),
        in_specs=[rowblk(dm), rowblk(RWKV_W), rowblk(S5_W), rowblk(S5_W), rowblk(FOX_W), rowblk(FOX_W),
                  memblk, memblk,
                  vec(S5_W), vec(S5_W), vec(S5_W), vec(FOX_W), vec(dm),
                  _vmem_full(), _vmem_full(), _vmem_full(), _vmem_full()],
        out_specs=rowblk(dm),
        out_shape=jax.ShapeDtypeStruct((n, dm), F32),
        compiler_params=_cparams(("parallel",)),
        name="mix_out",
    )(h, yr, ys, us, of, og, mem_k, mem_v, d, bglu, sg, fg, xg, wglu, wout, wxq, wxo)


def _ffn_kernel(h_ref, g_ref, w1_ref, w2_ref, fg_ref, o_ref, *, ff_chunk, final_norm):
    h = h_ref[...]
    hn = _rms(h, g_ref[...]).astype(BF16)
    acc = h
    for c0 in range(0, w1_ref.shape[1], ff_chunk):
        z = jnp.dot(hn, w1_ref[:, c0:c0 + ff_chunk], preferred_element_type=F32)
        z = jnp.maximum(z, 0.0)
        acc = acc + jnp.dot((z * z).astype(BF16), w2_ref[c0:c0 + ff_chunk, :], preferred_element_type=F32)
    if final_norm:
        acc = _rms(acc, fg_ref[...])
    o_ref[...] = acc


def _ffn(h, g, w1, w2, fg, final_norm, tm=512):
    n, dm = h.shape
    tm = min(tm, n)
    assert n % tm == 0
    vec = pl.BlockSpec((1, dm), lambda i: (0, 0))
    return pl.pallas_call(
        functools.partial(_ffn_kernel, ff_chunk=1024, final_norm=final_norm),
        grid=(n // tm,),
        in_specs=[pl.BlockSpec((tm, dm), lambda i: (i, 0)), vec, _vmem_full(), _vmem_full(), vec],
        out_specs=pl.BlockSpec((tm, dm), lambda i: (i, 0)),
        out_shape=jax.ShapeDtypeStruct((n, dm), F32),
        compiler_params=_cparams(("parallel",)),
        name="ffn",
    )(h, g.reshape(1, dm), w1, w2, fg.reshape(1, dm))


def _head_block_ones(width):
    i = jnp.arange(width) // HEAD_DIM
    return (i[:, None] == i[None, :]).astype(BF16)


def _rwkv_col_order():
    W = RWKV_W
    r = list(range(0, W))
    wl = list(range(W, W + DECAY_LORA))
    k = list(range(W + DECAY_LORA, 2 * W + DECAY_LORA))
    v = list(range(2 * W + DECAY_LORA, 3 * W + DECAY_LORA))
    al = list(range(3 * W + DECAY_LORA, 3 * W + DECAY_LORA + AAA_LORA))
    gl = list(range(3 * W + DECAY_LORA + AAA_LORA, RWKV_IN))
    return jnp.array(r + k + v + wl + al + gl, jnp.int32)


def _pad_heads(w, nheads):
    d = w.shape[0]
    w = w.reshape(d, nheads, HEAD_DIM)
    w = jnp.pad(w, ((0, 0), (0, 0), (0, LANES - HEAD_DIM)))
    return w.reshape(d, nheads * LANES)


def _pad_lanes(v, width=LANES):
    return jnp.pad(v, (0, width - v.shape[0])).reshape(1, width)


_QK_W = 2 * FOX_HEADS * LANES
_COLS = {}
_c = 0
for _name, _w in (("rwkv", RWKV_IN), ("s5", S5_W), ("qk", _QK_W), ("v", FOX_W), ("og", FOX_W), ("f", LANES)):
    _COLS[_name] = (_c, _c + _w)
    _c += _w
_COL_ORDER = ("rwkv", "s5", "qk", "v", "og", "f")


def _layout_w_in(w_in):
    o = RWKV_IN + S5_W
    w_r = w_in[:, :RWKV_IN][:, _rwkv_col_order()]
    w_s = w_in[:, RWKV_IN:o]
    w_q = _pad_heads(w_in[:, o:o + FOX_W], FOX_HEADS)
    w_k = _pad_heads(w_in[:, o + FOX_W:o + 2 * FOX_W], FOX_HEADS)
    w_v = w_in[:, o + 2 * FOX_W:o + 3 * FOX_W]
    w_g = w_in[:, o + 3 * FOX_W:o + 4 * FOX_W]
    w_f = jnp.pad(w_in[:, o + 4 * FOX_W:], ((0, 0), (0, LANES - FOX_HEADS)))
    return jnp.concatenate([w_r, w_s, w_q, w_k, w_v, w_g, w_f], axis=1).astype(BF16)


def kernel(x, mem, mem_norm_g, w_mem_kv, mix_norm_g, w_in, rwkv_mu, rwkv_w0, rwkv_w2, rwkv_a0, rwkv_a2, rwkv_g2, rwkv_k_k, rwkv_k_a, rwkv_r_k, rwkv_ln_w, rwkv_ln_b, s5_lam_re, s5_lam_im, s5_log_dt, s5_b_re, s5_b_im, s5_c_re, s5_c_im, s5_d, s5_w_glu, s5_b_glu, s5_out_g, fox_q_g, fox_k_g, fox_b_f, fox_out_g, w_out, xattn_norm_g, w_xq, w_xo, ffn_norm_g, w_ffn1, w_ffn2, final_norm_g):
    B, T, D = x.shape
    M = mem.shape[1]
    N = B * T
    depth = w_in.shape[0]
    W = RWKV_W
    nchunks = T // S5_CHUNK
    e_head = _head_block_ones(W)

    mem_kv = _norm_matmul(mem.reshape(B * M, D), mem_norm_g, w_mem_kv.astype(BF16),
                          [(0, D), (D, 2 * D)], [BF16, BF16], tm=256)
    mem_k = mem_kv[0].reshape(B, M, D)
    mem_v = mem_kv[1].reshape(B, M, D)

    col_ranges = [_COLS[n] for n in _COL_ORDER]
    col_dtypes = [F32] * len(col_ranges)
    rorder = _rwkv_col_order()

    h = x.reshape(N, D)
    for l in range(depth):
        p_r, p_s, p_qk, p_v, p_og, p_f = _norm_matmul(
            h, mix_norm_g[l], _layout_w_in(w_in[l]), col_ranges, col_dtypes)

        wwa = jnp.zeros((LANES, 2 * W), F32)
        wwa = wwa.at[:DECAY_LORA, :W].set(rwkv_w2[l]).at[DECAY_LORA:, W:].set(rwkv_a2[l]).astype(BF16)
        row = lambda v: v.reshape(1, -1)
        r, k, v, an, b, ld, g = _rwkv_prep(
            p_r.reshape(B, T, RWKV_IN), row(rwkv_mu[l][rorder]), row(rwkv_w0[l]), row(rwkv_a0[l]),
            row(rwkv_k_k[l]), row(rwkv_k_a[l]), wwa, rwkv_g2[l].astype(BF16), e_head)
        y_r = _rwkv_scan(r, k, v, an, b, ld, g, row(rwkv_ln_w[l]), row(rwkv_ln_b[l]),
                         row(rwkv_r_k[l])).reshape(N, W)

        m1, w1, w2, a1, a2 = _s5_weights(s5_lam_re[l], s5_lam_im[l], s5_log_dt[l], s5_b_re[l], s5_b_im[l],
                                         s5_c_re[l], s5_c_im[l], nchunks)
        u_t = p_s.reshape(B * nchunks, S5_CHUNK, S5_GROUPS, S5_GROUP_CH).transpose(2, 0, 1, 3)
        u_t = u_t.reshape(S5_GROUPS, B * nchunks, S5_CHUNK * S5_GROUP_CH)
        y_t = _s5_conv(u_t, m1, w1, w2, a1, a2, nchunks)
        y_s = y_t.reshape(S5_GROUPS, B * nchunks, S5_CHUNK, S5_GROUP_CH).transpose(1, 2, 0, 3).reshape(N, S5_W)

        fox_blk = min(FOX_BLOCK, T)
        q_aug, k_aug, v_t = _fox_prep(p_qk.reshape(B, T, _QK_W), p_v.reshape(B, T, FOX_W),
                                      p_f.reshape(B, T, LANES), _pad_lanes(fox_q_g[l]),
                                      _pad_lanes(fox_k_g[l]), _pad_lanes(fox_b_f[l]), fox_blk)
        o_f = _fox_attn(q_aug, k_aug, v_t, fox_blk).reshape(N, FOX_W)

        h = _mix_out(h, y_r, y_s, p_s, o_f, p_og, mem_k, mem_v,
                     row(s5_d[l]), row(s5_b_glu[l]), row(s5_out_g[l]), row(fox_out_g[l]), row(xattn_norm_g[l]),
                     s5_w_glu[l].astype(BF16), w_out[l].astype(BF16), w_xq[l].astype(BF16),
                     w_xo[l].astype(BF16), rows_per_batch=T)
        h = _ffn(h, ffn_norm_g[l], w_ffn1[l].astype(BF16), w_ffn2[l].astype(BF16), final_norm_g,
                 final_norm=(l == depth - 1))
    return h.reshape(B, T, D)
```

```python
import functools
import math

import jax
import jax.numpy as jnp
from jax import lax
from jax.experimental import pallas as pl
from jax.experimental.pallas import tpu as pltpu

F32 = jnp.float32
BF16 = jnp.bfloat16

D_MODEL = 1024
RWKV_HEADS = 4
HEAD_DIM = 64
RWKV_W = RWKV_HEADS * HEAD_DIM
DECAY_LORA = 64
AAA_LORA = 64
GATE_LORA = 128
RWKV_IN = 3 * RWKV_W + DECAY_LORA + AAA_LORA + GATE_LORA
S5_GROUPS = 16
S5_GROUP_CH = 16
S5_W = S5_GROUPS * S5_GROUP_CH
S5_STATE = 64
FOX_HEADS = 8
FOX_W = FOX_HEADS * HEAD_DIM
XATTN_HEADS = 4
XATTN_HEAD_DIM = D_MODEL // XATTN_HEADS
NORM_EPS = 1e-6
RWKV_LN_EPS = 64e-5

LANES = 128
RWKV_CHUNK = 64
S5_CHUNK = 16
FOX_AUG = 6
FOX_BLOCK = 512
VMEM_LIMIT = 56 * 1024 * 1024

NEG_BIG = -1e30
LOG2E = 1.4426950408889634


def _cparams(sem):
    return pltpu.CompilerParams(dimension_semantics=sem, vmem_limit_bytes=VMEM_LIMIT)


def _bdot(a, b):
    return jnp.dot(a.astype(BF16), b.astype(BF16), preferred_element_type=F32)


def _bdot_nt(a, b):
    return lax.dot_general(a.astype(BF16), b.astype(BF16), (((1,), (1,)), ((), ())),
                           preferred_element_type=F32)


def _split2(x):
    hi = x.astype(BF16)
    lo = (x - hi.astype(F32)).astype(BF16)
    return hi, lo


def _split3(x):
    hi = x.astype(BF16)
    r1 = x - hi.astype(F32)
    mid = r1.astype(BF16)
    lo = (r1 - mid.astype(F32)).astype(BF16)
    return hi, mid, lo


def _dot_exact_lhs(a_bf16, x):
    hi, mid, lo = _split3(x)
    d = lambda p: jnp.dot(a_bf16, p, preferred_element_type=F32)
    return d(hi) + d(mid) + d(lo)


def _dot_exact_rhs(x, b_bf16):
    hi, lo = _split2(x)
    d = lambda p: jnp.dot(p, b_bf16, preferred_element_type=F32)
    return d(hi) + d(lo)


def _mm3(a, b):
    ah, al = _split2(a)
    bh, bl = _split2(b)
    d = lambda p, q: jnp.dot(p, q, preferred_element_type=F32)
    return d(ah, bh) + d(ah, bl) + d(al, bh)


_mm_inv = _mm3


def _sigmoid(x):
    return 1.0 / (1.0 + jnp.exp(-x))


def _softplus(x):
    return jnp.maximum(x, 0.0) + jnp.log(1.0 + jnp.exp(-jnp.abs(x)))


def _rms(x, g, eps=NORM_EPS):
    return x * lax.rsqrt(jnp.mean(x * x, axis=-1, keepdims=True) + eps) * g


def _iota(shape, dim):
    return lax.broadcasted_iota(jnp.int32, shape, dim)


def _vmem_full():
    return pl.BlockSpec(memory_space=pltpu.VMEM)


def _norm_matmul_kernel(x_ref, g_ref, w_ref, *out_refs, col_ranges, col_chunk):
    xn = _rms(x_ref[...], g_ref[...]).astype(BF16)
    for o_ref, (c0, c1) in zip(out_refs, col_ranges):
        for cc in range(c0, c1, col_chunk):
            ce = min(cc + col_chunk, c1)
            o_ref[:, cc - c0:ce - c0] = jnp.dot(
                xn, w_ref[:, cc:ce], preferred_element_type=F32).astype(o_ref.dtype)


def _norm_matmul(x, g, w, col_ranges, out_dtypes, tm=512):
    n, d = x.shape
    tm = min(tm, n)
    assert n % tm == 0
    kern = functools.partial(_norm_matmul_kernel, col_ranges=tuple(col_ranges), col_chunk=512)
    return pl.pallas_call(
        kern,
        grid=(n // tm,),
        in_specs=[pl.BlockSpec((tm, d), lambda i: (i, 0)),
                  pl.BlockSpec((1, d), lambda i: (0, 0)),
                  _vmem_full()],
        out_specs=[pl.BlockSpec((tm, c1 - c0), lambda i: (i, 0)) for (c0, c1) in col_ranges],
        out_shape=[jax.ShapeDtypeStruct((n, c1 - c0), dt) for (c0, c1), dt in zip(col_ranges, out_dtypes)],
        compiler_params=_cparams(("parallel",)),
        name="norm_matmul",
    )(x, g.reshape(1, d), w)


def _rwkv_prep_kernel(p_ref, mu_ref, w0_ref, a0_ref, kk_ref, ka_ref, wwa_ref, g2_ref, e_ref,
                      r_out, k_out, v_out, an_out, b_out, ld_out, g_out, carry_ref):
    W = RWKV_W

    @pl.when(pl.program_id(1) == 0)
    def _():
        carry_ref[...] = jnp.zeros_like(carry_ref)

    p = p_ref[0]
    tb = p.shape[0]
    prev = carry_ref[...]
    rolled = pltpu.roll(p, 1, axis=0)
    shifted = jnp.where(_iota(p.shape, 0) == 0, prev, rolled)
    carry_ref[...] = p[tb - 1:tb, :]
    p = p + (shifted - p) * mu_ref[...]

    r = p[:, 0:W]
    k = p[:, W:2 * W]
    v = p[:, 2 * W:3 * W]
    wa_l = p[:, 3 * W:3 * W + LANES]
    g_l = p[:, 3 * W + LANES:]

    lora_in = jnp.where(_iota(wa_l.shape, 1) < DECAY_LORA, jnp.tanh(wa_l), wa_l)
    lora = _bdot(lora_in, wwa_ref[...])
    w = -_softplus(-(w0_ref[...] + lora[:, :W])) - 0.5
    a = _sigmoid(a0_ref[...] + lora[:, W:])
    g = _bdot(_sigmoid(g_l), g2_ref[...])

    kk = k * kk_ref[...]
    ss = _dot_exact_rhs(kk * kk, e_ref[...])
    kk = kk / jnp.maximum(jnp.sqrt(ss), 1e-12)
    k = k * (1.0 + (a - 1.0) * ka_ref[...])

    r_out[0] = r
    k_out[0] = k
    v_out[0] = v
    an_out[0] = -kk
    b_out[0] = kk * a
    ld_out[0] = -jnp.exp(w)
    g_out[0] = g


def _rwkv_prep(p, mu, w0, a0, k_k, k_a, wwa, g2, e_head, tb=256):
    B, T, C = p.shape
    tb = min(tb, T)
    W = RWKV_W
    vec = lambda n: pl.BlockSpec((1, n), lambda b, i: (0, 0))
    full = lambda s: pl.BlockSpec(s, lambda b, i: (0,) * len(s))
    out_spec = pl.BlockSpec((1, tb, W), lambda b, i: (b, i, 0))
    return pl.pallas_call(
        _rwkv_prep_kernel,
        grid=(B, T // tb),
        in_specs=[pl.BlockSpec((1, tb, C), lambda b, i: (b, i, 0)),
                  vec(C), vec(W), vec(W), vec(W), vec(W),
                  full(wwa.shape), full(g2.shape), full(e_head.shape)],
        out_specs=[out_spec] * 7,
        out_shape=[jax.ShapeDtypeStruct((B, T, W), F32)] * 7,
        scratch_shapes=[pltpu.VMEM((1, C), F32)],
        compiler_params=_cparams(("arbitrary", "arbitrary")),
        name="rwkv_prep",
    )(p, mu, w0, a0, k_k, k_a, wwa, g2, e_head)


def _stack_heads(x, head_mask):
    return jnp.where(head_mask, jnp.concatenate([x] * RWKV_HEADS, axis=0), 0.0)


def _rwkv_scan_kernel(r_ref, k_ref, v_ref, an_ref, b_ref, ld_ref, g_ref,
                      lnw_ref, lnb_ref, rk_ref, y_ref, state_ref):
    C = RWKV_CHUNK
    W = RWKV_W
    nb = r_ref.shape[0]

    @pl.when(pl.program_id(0) == 0)
    def _():
        state_ref[...] = jnp.zeros_like(state_ref)

    row = _iota((W, W), 0)
    col = _iota((W, W), 1)
    strict = col < row
    incl = col <= row
    eye = (col == row).astype(F32)
    head_mask = (row // C) == (col // HEAD_DIM)
    tri = (_iota((C, C), 1) <= _iota((C, C), 0)).astype(BF16)
    stack = lambda z: _stack_heads(z, head_mask)

    bs = range(nb)
    each = lambda f, *cols: [f(*xs) for xs in zip(*cols)]

    cum = [_dot_exact_lhs(tri, ld_ref[bi]) for bi in bs]
    p_inc = each(jnp.exp, cum)
    p_prev = [jnp.exp(cum[bi] - ld_ref[bi]) for bi in bs]
    p_inv = each(lambda c: jnp.exp(-c), cum)
    p_end = each(lambda p: p[C - 1:C, :], p_inc)

    rt = [stack(r_ref[bi] * p_inc[bi]) for bi in bs]
    at = [stack(an_ref[bi] * p_prev[bi]) for bi in bs]
    bt = [b_ref[bi] * p_inv[bi] for bi in bs]
    kt = [k_ref[bi] * p_inv[bi] for bi in bs]
    vs = [stack(v_ref[bi]) for bi in bs]
    bk = [jnp.concatenate([stack(bt[bi] * p_end[bi]), stack(kt[bi] * p_end[bi])], axis=0) for bi in bs]

    aa = [_bdot_nt(jnp.concatenate([at[bi], rt[bi]], axis=0),
                   jnp.concatenate([stack(bt[bi]), stack(kt[bi])], axis=0)) for bi in bs]
    n_ab = each(lambda z: jnp.where(strict, z[:W, :W], 0.0), aa)
    a_ak = each(lambda z: jnp.where(strict, z[:W, W:], 0.0), aa)
    a_rb = each(lambda z: jnp.where(incl, z[W:, :W], 0.0), aa)
    a_rk = each(lambda z: jnp.where(incl, z[W:, W:], 0.0), aa)

    inv = each(lambda n: eye + n, n_ab)
    pw = n_ab
    for _ in range(int(math.log2(C)) - 1):
        pw = each(_mm_inv, pw, pw)
        inv = each(lambda i, p: i + _mm_inv(p, i), inv, pw)

    av = each(_bdot, a_ak, vs)
    w_t = each(_mm_inv, inv, at)
    u_t = each(_mm_inv, inv, av)
    y_t = each(_bdot, a_rk, vs)

    H = [state_ref[bi] for bi in bs]
    x = [_bdot(jnp.concatenate([w_t[bi], rt[bi]], axis=0), H[bi]) for bi in bs]
    u = [x[bi][:W] + u_t[bi] for bi in bs]
    y = [x[bi][W:] + y_t[bi] + _bdot(a_rb[bi], u[bi]) for bi in bs]

    for bi in bs:
        uv = jnp.concatenate([u[bi], vs[bi]], axis=0)
        p_col = jnp.broadcast_to(p_end[bi], (8, W)).T[:, 0:1]
        state_ref[bi] = H[bi] * p_col + _bdot(bk[bi].T, uv)

    inv_n = 1.0 / HEAD_DIM
    for bi in bs:
        mean = jnp.sum(y[bi], axis=-1, keepdims=True) * inv_n
        yc = jnp.where(head_mask, y[bi] - mean, 0.0)
        var = jnp.sum(yc * yc, axis=-1, keepdims=True) * inv_n
        yn = jnp.where(head_mask, yc * lax.rsqrt(var + RWKV_LN_EPS) * lnw_ref[...] + lnb_ref[...], 0.0)
        bonus = jnp.sum(stack(r_ref[bi] * k_ref[bi] * rk_ref[...]), axis=-1, keepdims=True)
        tot = yn + bonus * vs[bi]
        out = tot[0:C]
        for h in range(1, RWKV_HEADS):
            out = out + tot[h * C:(h + 1) * C]
        y_ref[bi] = out * g_ref[bi]


def _rwkv_scan(r, k, v, an, b, ld, g, ln_w, ln_b, r_k):
    B, T, W = r.shape
    C = RWKV_CHUNK
    assert T % C == 0 and RWKV_HEADS * C == W
    blk = pl.BlockSpec((B, C, W), lambda i: (0, i, 0))
    vec = pl.BlockSpec((1, W), lambda i: (0, 0))
    return pl.pallas_call(
        _rwkv_scan_kernel,
        grid=(T // C,),
        in_specs=[blk] * 7 + [vec, vec, vec],
        out_specs=blk,
        out_shape=jax.ShapeDtypeStruct((B, T, W), F32),
        scratch_shapes=[pltpu.VMEM((B, W, W), F32)],
        compiler_params=_cparams(("arbitrary",)),
        name="rwkv_scan",
    )(r, k, v, an, b, ld, g, ln_w, ln_b, r_k)


def _s5_kernel(u_ref, m1_ref, w1_ref, w2_ref, a1_ref, a2_ref, y_ref, *, chunks_per_seq):
    u = u_ref[0].astype(BF16)
    yi = jnp.dot(u, m1_ref[0], preferred_element_type=F32)
    x = jnp.dot(u, w1_ref[0], preferred_element_type=F32)
    crow = _iota(x.shape, 0) & (chunks_per_seq - 1)
    a1 = a1_ref[0]
    a2 = a2_ref[0]
    for s in range(int(math.log2(chunks_per_seq))):
        sh = 1 << s
        xs = jnp.where(crow >= sh, pltpu.roll(x, sh, axis=0), 0.0)
        x = x + a1[s:s + 1, :] * xs + a2[s:s + 1, :] * pltpu.roll(xs, S5_STATE, axis=1)
    s0 = jnp.where(crow >= 1, pltpu.roll(x, 1, axis=0), 0.0)
    y_ref[0] = yi + _bdot(s0, w2_ref[0])


def _s5_conv(u_t, m1, w1, w2, a1, a2, chunks_per_seq):
    G, R, K = u_t.shape
    assert chunks_per_seq & (chunks_per_seq - 1) == 0
    spec = lambda a: pl.BlockSpec((1,) + a.shape[1:], lambda g: (g, 0, 0))
    return pl.pallas_call(
        functools.partial(_s5_kernel, chunks_per_seq=chunks_per_seq),
        grid=(G,),
        in_specs=[spec(u_t), spec(m1), spec(w1), spec(w2), spec(a1), spec(a2)],
        out_specs=spec(u_t),
        out_shape=jax.ShapeDtypeStruct((G, R, K), F32),
        compiler_params=_cparams(("parallel",)),
        name="s5_conv",
    )(u_t, m1, w1, w2, a1, a2)


def _s5_weights(lam_re, lam_im, log_dt, b_re, b_im, c_re, c_im, chunks_per_seq):
    hp = lax.Precision.HIGHEST
    L = S5_CHUNK
    dt = jnp.exp(log_dt)[:, None]
    lr = lam_re * dt
    li = lam_im * dt
    mag = jnp.exp(lr)
    ab_re = mag * jnp.cos(li)
    ab_im = mag * jnp.sin(li)
    den = lam_re * lam_re + lam_im * lam_im
    nr = ab_re - 1.0
    coef_re = (nr * lam_re + ab_im * lam_im) / den
    coef_im = (ab_im * lam_re - nr * lam_im) / den
    bb_re = coef_re[..., None] * b_re - coef_im[..., None] * b_im
    bb_im = coef_re[..., None] * b_im + coef_im[..., None] * b_re

    def power(n):
        n = n.astype(F32)[:, None, None]
        m = jnp.exp(n * lr[None])
        return m * jnp.cos(n * li[None]), m * jnp.sin(n * li[None])

    pw_re, pw_im = power(jnp.arange(L + 1))
    cp_re = c_re[None] * pw_re[:, :, None, :] - c_im[None] * pw_im[:, :, None, :]
    cp_im = c_re[None] * pw_im[:, :, None, :] + c_im[None] * pw_re[:, :, None, :]
    kern = (jnp.einsum('nghp,gpk->nghk', cp_re[:L], bb_re, precision=hp)
            - jnp.einsum('nghp,gpk->nghk', cp_im[:L], bb_im, precision=hp))
    s_idx = jnp.arange(L)[:, None]
    t_idx = jnp.arange(L)[None, :]
    lag = t_idx - s_idx
    kt = kern[jnp.clip(lag, 0, L - 1)]
    kt = jnp.where((lag >= 0)[:, :, None, None, None], kt, 0.0)
    G = lam_re.shape[0]
    H = S5_GROUP_CH
    m1 = kt.transpose(2, 0, 4, 1, 3).reshape(G, L * H, L * H)
    qr = pw_re[L - 1 - jnp.arange(L)]
    qi = pw_im[L - 1 - jnp.arange(L)]
    w1_re = qr[..., None] * bb_re[None] - qi[..., None] * bb_im[None]
    w1_im = qr[..., None] * bb_im[None] + qi[..., None] * bb_re[None]
    w1 = jnp.concatenate([w1_re, w1_im], axis=2)
    w1 = w1.transpose(1, 0, 3, 2).reshape(G, L * H, 2 * S5_STATE)
    w2 = jnp.concatenate([cp_re[1:], -cp_im[1:]], axis=3)
    w2 = w2.transpose(1, 3, 0, 2).reshape(G, 2 * S5_STATE, L * H)
    nsteps = int(math.log2(chunks_per_seq))
    sr, si = power(L * (2 ** jnp.arange(nsteps)))
    a1 = jnp.concatenate([sr, sr], axis=2).transpose(1, 0, 2)
    a2 = jnp.concatenate([-si, si], axis=2).transpose(1, 0, 2)
    pad = (-nsteps) % 8
    a1 = jnp.pad(a1, ((0, 0), (0, pad), (0, 0)))
    a2 = jnp.pad(a2, ((0, 0), (0, pad), (0, 0)))
    return m1.astype(BF16), w1.astype(BF16), w2.astype(BF16), a1, a2


def _fox_prep_kernel(qk_ref, v_ref, f_ref, qg_ref, kg_ref, bf_ref, q_out, k_out, vt_out, carry_ref):
    @pl.when(pl.program_id(1) == 0)
    def _():
        carry_ref[...] = jnp.zeros_like(carry_ref)

    tb = f_ref.shape[1]
    vt = v_ref[0].T
    ones_row = (_iota((LANES - HEAD_DIM, tb), 0) == 0).astype(F32)
    for h in range(FOX_HEADS):
        vt_out[0, h, 0] = jnp.concatenate(
            [vt[h * HEAD_DIM:(h + 1) * HEAD_DIM, :], ones_row], axis=0).astype(BF16)

    x = f_ref[0] + bf_ref[...]
    logf = jnp.minimum(x, 0.0) - jnp.log(1.0 + jnp.exp(-jnp.abs(x)))
    tri = (_iota((tb, tb), 1) <= _iota((tb, tb), 0)).astype(BF16)
    cum = _dot_exact_lhs(tri, logf) + carry_ref[...]
    carry_ref[...] = cum[tb - 1:tb, :]
    c2 = cum * LOG2E
    c_hi = c2.astype(BF16).astype(F32)
    r1 = c2 - c_hi
    c_mid = r1.astype(BF16).astype(F32)
    c_lo = (r1 - c_mid).astype(BF16).astype(F32)

    lane = _iota((tb, LANES), 1)
    D = HEAD_DIM
    scale = LOG2E / math.sqrt(D)
    for h in range(FOX_HEADS):
        hi = jnp.broadcast_to(c_hi[:, h:h + 1], (tb, LANES))
        mid = jnp.broadcast_to(c_mid[:, h:h + 1], (tb, LANES))
        lo = jnp.broadcast_to(c_lo[:, h:h + 1], (tb, LANES))
        q = qk_ref[0, :, h * LANES:(h + 1) * LANES]
        k = qk_ref[0, :, (FOX_HEADS + h) * LANES:(FOX_HEADS + h + 1) * LANES]
        qn = q * lax.rsqrt(jnp.sum(q * q, axis=-1, keepdims=True) * (1.0 / D) + NORM_EPS) * qg_ref[...] * scale
        kn = k * lax.rsqrt(jnp.sum(k * k, axis=-1, keepdims=True) * (1.0 / D) + NORM_EPS) * kg_ref[...]
        q_aug = jnp.where(lane < D, qn,
                          jnp.where(lane < D + 3, 1.0,
                                    jnp.where(lane == D + 3, hi,
                                              jnp.where(lane == D + 4, mid,
                                                        jnp.where(lane == D + 5, lo, 0.0)))))
        k_aug = jnp.where(lane < D, kn,
                          jnp.where(lane == D, -hi,
                                    jnp.where(lane == D + 1, -mid,
                                              jnp.where(lane == D + 2, -lo,
                                                        jnp.where(lane < D + FOX_AUG, 1.0, 0.0)))))
        q_out[0, :, h * LANES:(h + 1) * LANES] = q_aug.astype(BF16)
        k_out[0, :, h * LANES:(h + 1) * LANES] = k_aug.astype(BF16)


def _fox_prep(qk, v, f, q_g, k_g, b_f, tb):
    B, T, _ = qk.shape
    HW = FOX_HEADS * LANES
    vec = pl.BlockSpec((1, LANES), lambda b, i: (0, 0))
    out_spec = pl.BlockSpec((1, tb, HW), lambda b, i: (b, i, 0))
    return pl.pallas_call(
        _fox_prep_kernel,
        grid=(B, T // tb),
        in_specs=[pl.BlockSpec((1, tb, 2 * HW), lambda b, i: (b, i, 0)),
                  pl.BlockSpec((1, tb, FOX_W), lambda b, i: (b, i, 0)),
                  pl.BlockSpec((1, tb, LANES), lambda b, i: (b, i, 0)),
                  vec, vec, vec],
        out_specs=[out_spec, out_spec,
                   pl.BlockSpec((1, FOX_HEADS, 1, LANES, tb), lambda b, i: (b, 0, i, 0, 0))],
        out_shape=[jax.ShapeDtypeStruct((B, T, HW), BF16)] * 2
        + [jax.ShapeDtypeStruct((B, FOX_HEADS, T // tb, LANES, tb), BF16)],
        scratch_shapes=[pltpu.VMEM((1, LANES), F32)],
        compiler_params=_cparams(("arbitrary", "arbitrary")),
        name="fox_prep",
    )(qk, v, f, q_g, k_g, b_f)


def _fox_attn_kernel(q_ref, k_ref, vt_ref, o_ref, m_ref, acc_ref, sa_ref, sb_ref, *, tq):
    qi = pl.program_id(2)
    m_ref[...] = jnp.full(m_ref.shape, NEG_BIG, F32)
    acc_ref[...] = jnp.zeros_like(acc_ref)

    tqh = tq // 2

    chains = [(hh, half) for hh in range(2) for half in range(2)]

    def scores(j, s_ref):
        start = pl.multiple_of(j * tq, tq)
        for c, (hh, half) in enumerate(chains):
            hs = slice(hh * LANES, (hh + 1) * LANES)
            qs = slice(half * tqh, (half + 1) * tqh)
            s_ref[c] = lax.dot_general(k_ref[0, pl.ds(start, tq), hs], q_ref[0, qs, hs],
                                       (((1,), (1,)), ((), ())),
                                       preferred_element_type=F32)

    def softmax_pv(j, s_ref, masked):
        for c, (hh, half) in enumerate(chains):
            st = s_ref[c]
            qs = slice(half * tqh, (half + 1) * tqh)
            if masked:
                keep = _iota(st.shape, 0) <= _iota(st.shape, 1) + half * tqh
                st = jnp.where(keep, st, NEG_BIG)
            m_old = m_ref[hh, :, qs]
            m_new = jnp.maximum(m_old, jnp.max(st, axis=0, keepdims=True))
            p = jnp.exp2(st - m_new).astype(BF16)
            alpha = jnp.exp2(m_old - m_new)
            m_ref[hh, :, qs] = m_new
            acc_ref[hh, :, qs] = alpha * acc_ref[hh, :, qs] + jnp.dot(vt_ref[0, hh, j], p,
                                                                       preferred_element_type=F32)

    def body(i, carry):
        j = 2 * i
        scores(j + 1, sb_ref)
        softmax_pv(j, sa_ref, False)
        scores(j + 2, sa_ref)
        softmax_pv(j + 1, sb_ref, False)
        return carry

    scores(0, sa_ref)
    lax.fori_loop(0, qi // 2, body, 0)

    @pl.when(qi % 2 == 0)
    def _():
        softmax_pv(qi, sa_ref, True)

    @pl.when(qi % 2 == 1)
    def _():
        scores(qi, sb_ref)
        softmax_pv(qi - 1, sa_ref, False)
        softmax_pv(qi, sb_ref, True)

    outs = []
    for hh in range(2):
        acc = acc_ref[hh]
        outs.append(acc[:HEAD_DIM, :] / acc[HEAD_DIM:HEAD_DIM + 1, :])
    o_ref[0] = jnp.concatenate(outs, axis=0).T


def _fox_attn(q_aug, k_aug, vt, tq):
    B, T, _ = q_aug.shape
    npairs = FOX_HEADS // 2
    return pl.pallas_call(
        functools.partial(_fox_attn_kernel, tq=tq),
        grid=(B, npairs, T // tq),
        in_specs=[pl.BlockSpec((1, tq, 2 * LANES), lambda b, h, i: (b, i, h)),
                  pl.BlockSpec((1, T, 2 * LANES), lambda b, h, i: (b, 0, h)),
                  pl.BlockSpec((1, 2, T // tq, LANES, tq), lambda b, h, i: (b, h, 0, 0, 0))],
        out_specs=pl.BlockSpec((1, tq, LANES), lambda b, h, i: (b, i, h)),
        out_shape=jax.ShapeDtypeStruct((B, T, FOX_W), F32),
        scratch_shapes=[pltpu.VMEM((2, 1, tq), F32), pltpu.VMEM((2, LANES, tq), F32),
                        pltpu.VMEM((4, tq, tq // 2), F32), pltpu.VMEM((4, tq, tq // 2), F32)],
        compiler_params=_cparams(("parallel", "parallel", "arbitrary")),
        name="fox_attn",
    )(q_aug, k_aug, vt)


def _gelu_tanh(x):
    return 0.5 * x * (1.0 + jnp.tanh(math.sqrt(2.0 / math.pi) * (x + 0.044715 * (x * x * x))))


def _mix_out_kernel(h_ref, yr_ref, ys_ref, us_ref, of_ref, og_ref, mk_ref, mv_ref,
                    d_ref, bglu_ref, sg_ref, fg_ref, xg_ref,
                    wglu_ref, wout_ref, wxq_ref, wxo_ref, o_ref):
    h = h_ref[...]
    ys = _gelu_tanh(ys_ref[...] + d_ref[...] * us_ref[...])
    ys = ys * _sigmoid(_bdot(ys, wglu_ref[...]) + bglu_ref[...])
    ys = _rms(ys, sg_ref[...])
    yf = _rms(of_ref[...] * _sigmoid(og_ref[...]), fg_ref[...])
    r0, r1 = RWKV_W, RWKV_W + S5_W
    h = h + (_bdot(yr_ref[...], wout_ref[0:r0, :]) + _bdot(ys, wout_ref[r0:r1, :])
             + _bdot(yf, wout_ref[r1:, :]))
    q = jnp.dot(_rms(h, xg_ref[...]).astype(BF16), wxq_ref[...], preferred_element_type=F32)
    scale = 1.0 / math.sqrt(XATTN_HEAD_DIM)
    upd = jnp.zeros_like(h)
    for hd in range(XATTN_HEADS):
        cs = slice(hd * XATTN_HEAD_DIM, (hd + 1) * XATTN_HEAD_DIM)
        s = _bdot_nt(q[:, cs], mk_ref[0, :, cs]) * scale
        s = s - jnp.max(s, axis=-1, keepdims=True)
        p = jnp.exp(s)
        p = p / jnp.sum(p, axis=-1, keepdims=True)
        o = _bdot(p, mv_ref[0, :, cs])
        upd = upd + _bdot(o, wxo_ref[cs, :])
    o_ref[...] = h + upd


def _mix_out(h, yr, ys, us, of, og, mem_k, mem_v, d, bglu, sg, fg, xg, wglu, wout, wxq, wxo,
             rows_per_batch, tm=512):
    n, dm = h.shape
    tm = min(tm, rows_per_batch)
    assert rows_per_batch % tm == 0
    bpb = rows_per_batch // tm
    rowblk = lambda w: pl.BlockSpec((tm, w), lambda i: (i, 0))
    vec = lambda w: pl.BlockSpec((1, w), lambda i: (0, 0))
    memblk = pl.BlockSpec((1,) + mem_k.shape[1:], lambda i: (i // bpb, 0, 0))
    return pl.pallas_call(
        _mix_out_kernel,
        grid=(n // tm,),
        in_specs=[rowblk(dm), rowblk(RWKV_W), rowblk(S5_W), rowblk(S5_W), rowblk(FOX_W), rowblk(FOX_W),
                  memblk, memblk,
                  vec(S5_W), vec(S5_W), vec(S5_W), vec(FOX_W), vec(dm),
                  _vmem_full(), _vmem_full(), _vmem_full(), _vmem_full()],
        out_specs=rowblk(dm),
        out_shape=jax.ShapeDtypeStruct((n, dm), F32),
        compiler_params=_cparams(("parallel",)),
        name="mix_out",
    )(h, yr, ys, us, of, og, mem_k, mem_v, d, bglu, sg, fg, xg, wglu, wout, wxq, wxo)


def _ffn_kernel(h_ref, g_ref, w1_ref, w2_ref, fg_ref, o_ref, *, ff_chunk, final_norm):
    h = h_ref[...]
    hn = _rms(h, g_ref[...]).astype(BF16)
    acc = h
    for c0 in range(0, w1_ref.shape[1], ff_chunk):
        z = jnp.dot(hn, w1_ref[:, c0:c0 + ff_chunk], preferred_element_type=F32)
        z = jnp.maximum(z, 0.0)
        acc = acc + jnp.dot((z * z).astype(BF16), w2_ref[c0:c0 + ff_chunk, :], preferred_element_type=F32)
    if final_norm:
        acc = _rms(acc, fg_ref[...])
    o_ref[...] = acc


def _ffn(h, g, w1, w2, fg, final_norm, tm=512):
    n, dm = h.shape
    tm = min(tm, n)
    assert n % tm == 0
    vec = pl.BlockSpec((1, dm), lambda i: (0, 0))
    return pl.pallas_call(
        functools.partial(_ffn_kernel, ff_chunk=1024, final_norm=final_norm),
        grid=(n // tm,),
        in_specs=[pl.BlockSpec((tm, dm), lambda i: (i, 0)), vec, _vmem_full(), _vmem_full(), vec],
        out_specs=pl.BlockSpec((tm, dm), lambda i: (i, 0)),
        out_shape=jax.ShapeDtypeStruct((n, dm), F32),
        compiler_params=_cparams(("parallel",)),
        name="ffn",
    )(h, g.reshape(1, dm), w1, w2, fg.reshape(1, dm))


def _head_block_ones(width):
    i = jnp.arange(width) // HEAD_DIM
    return (i[:, None] == i[None, :]).astype(BF16)


def _rwkv_col_order():
    W = RWKV_W
    r = list(range(0, W))
    wl = list(range(W, W + DECAY_LORA))
    k = list(range(W + DECAY_LORA, 2 * W + DECAY_LORA))
    v = list(range(2 * W + DECAY_LORA, 3 * W + DECAY_LORA))
    al = list(range(3 * W + DECAY_LORA, 3 * W + DECAY_LORA + AAA_LORA))
    gl = list(range(3 * W + DECAY_LORA + AAA_LORA, RWKV_IN))
    return jnp.array(r + k + v + wl + al + gl, jnp.int32)


def _pad_heads(w, nheads):
    d = w.shape[0]
    w = w.reshape(d, nheads, HEAD_DIM)
    w = jnp.pad(w, ((0, 0), (0, 0), (0, LANES - HEAD_DIM)))
    return w.reshape(d, nheads * LANES)


def _pad_lanes(v, width=LANES):
    return jnp.pad(v, (0, width - v.shape[0])).reshape(1, width)


_QK_W = 2 * FOX_HEADS * LANES
_COLS = {}
_c = 0
for _name, _w in (("rwkv", RWKV_IN), ("s5", S5_W), ("qk", _QK_W), ("v", FOX_W), ("og", FOX_W), ("f", LANES)):
    _COLS[_name] = (_c, _c + _w)
    _c += _w
_COL_ORDER = ("rwkv", "s5", "qk", "v", "og", "f")


def _layout_w_in(w_in):
    o = RWKV_IN + S5_W
    w_r = w_in[:, :RWKV_IN][:, _rwkv_col_order()]
    w_s = w_in[:, RWKV_IN:o]
    w_q = _pad_heads(w_in[:, o:o + FOX_W], FOX_HEADS)
    w_k = _pad_heads(w_in[:, o + FOX_W:o + 2 * FOX_W], FOX_HEADS)
    w_v = w_in[:, o + 2 * FOX_W:o + 3 * FOX_W]
    w_g = w_in[:, o + 3 * FOX_W:o + 4 * FOX_W]
    w_f = jnp.pad(w_in[:, o + 4 * FOX_W:], ((0, 0), (0, LANES - FOX_HEADS)))
    return jnp.concatenate([w_r, w_s, w_q, w_k, w_v, w_g, w_f], axis=1).astype(BF16)


def kernel(x, mem, mem_norm_g, w_mem_kv, mix_norm_g, w_in, rwkv_mu, rwkv_w0, rwkv_w2, rwkv_a0, rwkv_a2, rwkv_g2, rwkv_k_k, rwkv_k_a, rwkv_r_k, rwkv_ln_w, rwkv_ln_b, s5_lam_re, s5_lam_im, s5_log_dt, s5_b_re, s5_b_im, s5_c_re, s5_c_im, s5_d, s5_w_glu, s5_b_glu, s5_out_g, fox_q_g, fox_k_g, fox_b_f, fox_out_g, w_out, xattn_norm_g, w_xq, w_xo, ffn_norm_g, w_ffn1, w_ffn2, final_norm_g):
    B, T, D = x.shape
    M = mem.shape[1]
    N = B * T
    depth = w_in.shape[0]
    W = RWKV_W
    nchunks = T // S5_CHUNK
    e_head = _head_block_ones(W)

    mem_kv = _norm_matmul(mem.reshape(B * M, D), mem_norm_g, w_mem_kv.astype(BF16),
                          [(0, D), (D, 2 * D)], [BF16, BF16], tm=256)
    mem_k = mem_kv[0].reshape(B, M, D)
    mem_v = mem_kv[1].reshape(B, M, D)

    col_ranges = [_COLS[n] for n in _COL_ORDER]
    col_dtypes = [F32] * len(col_ranges)
    rorder = _rwkv_col_order()

    h = x.reshape(N, D)
    for l in range(depth):
        p_r, p_s, p_qk, p_v, p_og, p_f = _norm_matmul(
            h, mix_norm_g[l], _layout_w_in(w_in[l]), col_ranges, col_dtypes)

        wwa = jnp.zeros((LANES, 2 * W), F32)
        wwa = wwa.at[:DECAY_LORA, :W].set(rwkv_w2[l]).at[DECAY_LORA:, W:].set(rwkv_a2[l]).astype(BF16)
        row = lambda v: v.reshape(1, -1)
        r, k, v, an, b, ld, g = _rwkv_prep(
            p_r.reshape(B, T, RWKV_IN), row(rwkv_mu[l][rorder]), row(rwkv_w0[l]), row(rwkv_a0[l]),
            row(rwkv_k_k[l]), row(rwkv_k_a[l]), wwa, rwkv_g2[l].astype(BF16), e_head)
        y_r = _rwkv_scan(r, k, v, an, b, ld, g, row(rwkv_ln_w[l]), row(rwkv_ln_b[l]),
                         row(rwkv_r_k[l])).reshape(N, W)

        m1, w1, w2, a1, a2 = _s5_weights(s5_lam_re[l], s5_lam_im[l], s5_log_dt[l], s5_b_re[l], s5_b_im[l],
                                         s5_c_re[l], s5_c_im[l], nchunks)
        u_t = p_s.reshape(B * nchunks, S5_CHUNK, S5_GROUPS, S5_GROUP_CH).transpose(2, 0, 1, 3)
        u_t = u_t.reshape(S5_GROUPS, B * nchunks, S5_CHUNK * S5_GROUP_CH)
        y_t = _s5_conv(u_t, m1, w1, w2, a1, a2, nchunks)
        y_s = y_t.reshape(S5_GROUPS, B * nchunks, S5_CHUNK, S5_GROUP_CH).transpose(1, 2, 0, 3).reshape(N, S5_W)

        fox_blk = min(FOX_BLOCK, T)
        q_aug, k_aug, v_t = _fox_prep(p_qk.reshape(B, T, _QK_W), p_v.reshape(B, T, FOX_W),
                                      p_f.reshape(B, T, LANES), _pad_lanes(fox_q_g[l]),
                                      _pad_lanes(fox_k_g[l]), _pad_lanes(fox_b_f[l]), fox_blk)
        o_f = _fox_attn(q_aug, k_aug, v_t, fox_blk).reshape(N, FOX_W)

        h = _mix_out(h, y_r, y_s, p_s, o_f, p_og, mem_k, mem_v,
                     row(s5_d[l]), row(s5_b_glu[l]), row(s5_out_g[l]), row(fox_out_g[l]), row(xattn_norm_g[l]),
                     s5_w_glu[l].astype(BF16), w_out[l].astype(BF16), w_xq[l].astype(BF16),
                     w_xo[l].astype(BF16), rows_per_batch=T)
        h = _ffn(h, ffn_norm_g[l], w_ffn1[l].astype(BF16), w_ffn2[l].astype(BF16), final_norm_g,
                 final_norm=(l == depth - 1))
    return h.reshape(B, T, D)
```

```python
import functools
import math

import jax
import jax.numpy as jnp
from jax import lax
from jax.experimental import pallas as pl
from jax.experimental.pallas import tpu as pltpu

F32 = jnp.float32
BF16 = jnp.bfloat16

D_MODEL = 1024
RWKV_HEADS = 4
HEAD_DIM = 64
RWKV_W = RWKV_HEADS * HEAD_DIM
DECAY_LORA = 64
AAA_LORA = 64
GATE_LORA = 128
RWKV_IN = 3 * RWKV_W + DECAY_LORA + AAA_LORA + GATE_LORA
S5_GROUPS = 16
S5_GROUP_CH = 16
S5_W = S5_GROUPS * S5_GROUP_CH
S5_STATE = 64
FOX_HEADS = 8
FOX_W = FOX_HEADS * HEAD_DIM
XATTN_HEADS = 4
XATTN_HEAD_DIM = D_MODEL // XATTN_HEADS
NORM_EPS = 1e-6
RWKV_LN_EPS = 64e-5

LANES = 128
RWKV_CHUNK = 64
S5_CHUNK = 16
FOX_AUG = 6
FOX_BLOCK = 512
VMEM_LIMIT = 56 * 1024 * 1024

NEG_BIG = -1e30
LOG2E = 1.4426950408889634


def _cparams(sem):
    return pltpu.CompilerParams(dimension_semantics=sem, vmem_limit_bytes=VMEM_LIMIT)


def _bdot(a, b):
    return jnp.dot(a.astype(BF16), b.astype(BF16), preferred_element_type=F32)


def _bdot_nt(a, b):
    return lax.dot_general(a.astype(BF16), b.astype(BF16), (((1,), (1,)), ((), ())),
                           preferred_element_type=F32)


def _split2(x):
    hi = x.astype(BF16)
    lo = (x - hi.astype(F32)).astype(BF16)
    return hi, lo


def _split3(x):
    hi = x.astype(BF16)
    r1 = x - hi.astype(F32)
    mid = r1.astype(BF16)
    lo = (r1 - mid.astype(F32)).astype(BF16)
    return hi, mid, lo


def _dot_exact_lhs(a_bf16, x):
    hi, mid, lo = _split3(x)
    d = lambda p: jnp.dot(a_bf16, p, preferred_element_type=F32)
    return d(hi) + d(mid) + d(lo)


def _dot_exact_rhs(x, b_bf16):
    hi, lo = _split2(x)
    d = lambda p: jnp.dot(p, b_bf16, preferred_element_type=F32)
    return d(hi) + d(lo)


def _mm3(a, b):
    ah, al = _split2(a)
    bh, bl = _split2(b)
    d = lambda p, q: jnp.dot(p, q, preferred_element_type=F32)
    return d(ah, bh) + d(ah, bl) + d(al, bh)


def _pack_heads(x):
    n = x.shape[0]
    d = HEAD_DIM
    hi = x.astype(BF16)
    lo_f = x - hi.astype(F32)
    lo = lo_f.astype(BF16)
    rows = lambda z, s: jnp.concatenate([z[n - s:], z[:n - s]], axis=0)
    lhs = (hi + pltpu.roll(lo_f, d, axis=1).astype(BF16)
           + jnp.concatenate([hi[:, n - 2 * d:], hi[:, :n - 2 * d]], axis=1))
    rhs = hi + rows(hi, d) + rows(lo, 2 * d)
    return lhs, rhs


def _mm_packed(lhs, rhs, head_mask):
    return jnp.where(head_mask, jnp.dot(lhs, rhs, preferred_element_type=F32), 0.0)


def _sigmoid(x):
    return 1.0 / (1.0 + jnp.exp(-x))


def _softplus(x):
    return jnp.maximum(x, 0.0) + jnp.log(1.0 + jnp.exp(-jnp.abs(x)))


def _rms(x, g, eps=NORM_EPS):
    return x * lax.rsqrt(jnp.mean(x * x, axis=-1, keepdims=True) + eps) * g


def _iota(shape, dim):
    return lax.broadcasted_iota(jnp.int32, shape, dim)


def _vmem_full():
    return pl.BlockSpec(memory_space=pltpu.VMEM)


def _norm_matmul_kernel(x_ref, g_ref, w_ref, *out_refs, col_ranges, col_chunk):
    xn = _rms(x_ref[...], g_ref[...]).astype(BF16)
    for o_ref, (c0, c1) in zip(out_refs, col_ranges):
        for cc in range(c0, c1, col_chunk):
            ce = min(cc + col_chunk, c1)
            o_ref[:, cc - c0:ce - c0] = jnp.dot(
                xn, w_ref[:, cc:ce], preferred_element_type=F32).astype(o_ref.dtype)


def _norm_matmul(x, g, w, col_ranges, out_dtypes, tm=512):
    n, d = x.shape
    tm = min(tm, n)
    assert n % tm == 0
    kern = functools.partial(_norm_matmul_kernel, col_ranges=tuple(col_ranges), col_chunk=512)
    return pl.pallas_call(
        kern,
        grid=(n // tm,),
        in_specs=[pl.BlockSpec((tm, d), lambda i: (i, 0)),
                  pl.BlockSpec((1, d), lambda i: (0, 0)),
                  _vmem_full()],
        out_specs=[pl.BlockSpec((tm, c1 - c0), lambda i: (i, 0)) for (c0, c1) in col_ranges],
        out_shape=[jax.ShapeDtypeStruct((n, c1 - c0), dt) for (c0, c1), dt in zip(col_ranges, out_dtypes)],
        compiler_params=_cparams(("parallel",)),
        name="norm_matmul",
    )(x, g.reshape(1, d), w)


def _rwkv_prep_kernel(p_ref, mu_ref, w0_ref, a0_ref, kk_ref, ka_ref, wwa_ref, g2_ref, e_ref,
                      r_out, k_out, v_out, an_out, b_out, ld_out, g_out, carry_ref):
    W = RWKV_W

    @pl.when(pl.program_id(1) == 0)
    def _():
        carry_ref[...] = jnp.zeros_like(carry_ref)

    p = p_ref[0]
    tb = p.shape[0]
    prev = carry_ref[...]
    rolled = pltpu.roll(p, 1, axis=0)
    shifted = jnp.where(_iota(p.shape, 0) == 0, prev, rolled)
    carry_ref[...] = p[tb - 1:tb, :]
    p = p + (shifted - p) * mu_ref[...]

    r = p[:, 0:W]
    k = p[:, W:2 * W]
    v = p[:, 2 * W:3 * W]
    wa_l = p[:, 3 * W:3 * W + LANES]
    g_l = p[:, 3 * W + LANES:]

    lora_in = jnp.where(_iota(wa_l.shape, 1) < DECAY_LORA, jnp.tanh(wa_l), wa_l)
    lora = _bdot(lora_in, wwa_ref[...])
    w = -_softplus(-(w0_ref[...] + lora[:, :W])) - 0.5
    a = _sigmoid(a0_ref[...] + lora[:, W:])
    g = _bdot(_sigmoid(g_l), g2_ref[...])

    kk = k * kk_ref[...]
    ss = _dot_exact_rhs(kk * kk, e_ref[...])
    kk = kk / jnp.maximum(jnp.sqrt(ss), 1e-12)
    k = k * (1.0 + (a - 1.0) * ka_ref[...])

    r_out[0] = r
    k_out[0] = k
    v_out[0] = v
    an_out[0] = -kk
    b_out[0] = kk * a
    ld_out[0] = -jnp.exp(w)
    g_out[0] = g


def _rwkv_prep(p, mu, w0, a0, k_k, k_a, wwa, g2, e_head, tb=256):
    B, T, C = p.shape
    tb = min(tb, T)
    W = RWKV_W
    vec = lambda n: pl.BlockSpec((1, n), lambda b, i: (0, 0))
    full = lambda s: pl.BlockSpec(s, lambda b, i: (0,) * len(s))
    out_spec = pl.BlockSpec((1, tb, W), lambda b, i: (b, i, 0))
    return pl.pallas_call(
        _rwkv_prep_kernel,
        grid=(B, T // tb),
        in_specs=[pl.BlockSpec((1, tb, C), lambda b, i: (b, i, 0)),
                  vec(C), vec(W), vec(W), vec(W), vec(W),
                  full(wwa.shape), full(g2.shape), full(e_head.shape)],
        out_specs=[out_spec] * 7,
        out_shape=[jax.ShapeDtypeStruct((B, T, W), F32)] * 7,
        scratch_shapes=[pltpu.VMEM((1, C), F32)],
        compiler_params=_cparams(("arbitrary", "arbitrary")),
        name="rwkv_prep",
    )(p, mu, w0, a0, k_k, k_a, wwa, g2, e_head)


def _stack_heads(x, head_mask):
    return jnp.where(head_mask, jnp.concatenate([x] * RWKV_HEADS, axis=0), 0.0)


def _rwkv_scan_kernel(r_ref, k_ref, v_ref, an_ref, b_ref, ld_ref, g_ref,
                      lnw_ref, lnb_ref, rk_ref, y_ref, state_ref):
    C = RWKV_CHUNK
    W = RWKV_W
    nb = r_ref.shape[0]

    @pl.when(pl.program_id(0) == 0)
    def _():
        state_ref[...] = jnp.zeros_like(state_ref)

    row = _iota((W, W), 0)
    col = _iota((W, W), 1)
    strict = col < row
    incl = col <= row
    eye = (col == row).astype(F32)
    head_mask = (row // C) == (col // HEAD_DIM)
    tri = (_iota((C, C), 1) <= _iota((C, C), 0)).astype(BF16)
    stack = lambda z: _stack_heads(z, head_mask)
    mm_packed = lambda p, q: _mm_packed(p, q, head_mask)

    bs = range(nb)
    each = lambda f, *cols: [f(*xs) for xs in zip(*cols)]

    cum = [_dot_exact_lhs(tri, ld_ref[bi]) for bi in bs]
    p_inc = each(jnp.exp, cum)
    p_prev = [jnp.exp(cum[bi] - ld_ref[bi]) for bi in bs]
    p_inv = each(lambda c: jnp.exp(-c), cum)
    p_end = each(lambda p: p[C - 1:C, :], p_inc)

    rt = [stack(r_ref[bi] * p_inc[bi]) for bi in bs]
    at = [stack(an_ref[bi] * p_prev[bi]) for bi in bs]
    bt = [b_ref[bi] * p_inv[bi] for bi in bs]
    kt = [k_ref[bi] * p_inv[bi] for bi in bs]
    vs = [stack(v_ref[bi]) for bi in bs]
    bk = [jnp.concatenate([stack(bt[bi] * p_end[bi]), stack(kt[bi] * p_end[bi])], axis=0) for bi in bs]

    aa = [_bdot_nt(jnp.concatenate([at[bi], rt[bi]], axis=0),
                   jnp.concatenate([stack(bt[bi]), stack(kt[bi])], axis=0)) for bi in bs]
    n_ab = each(lambda z: jnp.where(strict, z[:W, :W], 0.0), aa)
    a_ak = each(lambda z: jnp.where(strict, z[:W, W:], 0.0), aa)
    a_rb = each(lambda z: jnp.where(incl, z[W:, :W], 0.0), aa)
    a_rk = each(lambda z: jnp.where(incl, z[W:, W:], 0.0), aa)

    inv = each(lambda n: eye + n, n_ab)
    pw_l, pw_r = zip(*each(_pack_heads, n_ab))
    for _ in range(int(math.log2(C)) - 1):
        pw = each(mm_packed, pw_l, pw_r)
        pw_l, pw_r = zip(*each(_pack_heads, pw))
        inv = each(lambda i, pl_: i + mm_packed(pl_, _pack_heads(i)[1]), inv, pw_l)

    av = each(_bdot, a_ak, vs)
    inv_l = each(lambda i: _pack_heads(i)[0], inv)
    w_t = each(lambda il, z: mm_packed(il, _pack_heads(z)[1]), inv_l, at)
    u_t = each(lambda il, z: mm_packed(il, _pack_heads(z)[1]), inv_l, av)
    y_t = each(_bdot, a_rk, vs)

    H = [state_ref[bi] for bi in bs]
    x = [_bdot(jnp.concatenate([w_t[bi], rt[bi]], axis=0), H[bi]) for bi in bs]
    u = [x[bi][:W] + u_t[bi] for bi in bs]
    y = [x[bi][W:] + y_t[bi] + _bdot(a_rb[bi], u[bi]) for bi in bs]

    for bi in bs:
        uv = jnp.concatenate([u[bi], vs[bi]], axis=0)
        p_col = jnp.broadcast_to(p_end[bi], (8, W)).T[:, 0:1]
        state_ref[bi] = H[bi] * p_col + _bdot(bk[bi].T, uv)

    inv_n = 1.0 / HEAD_DIM
    for bi in bs:
        mean = jnp.sum(y[bi], axis=-1, keepdims=True) * inv_n
        yc = jnp.where(head_mask, y[bi] - mean, 0.0)
        var = jnp.sum(yc * yc, axis=-1, keepdims=True) * inv_n
        yn = jnp.where(head_mask, yc * lax.rsqrt(var + RWKV_LN_EPS) * lnw_ref[...] + lnb_ref[...], 0.0)
        bonus = jnp.sum(stack(r_ref[bi] * k_ref[bi] * rk_ref[...]), axis=-1, keepdims=True)
        tot = yn + bonus * vs[bi]
        out = tot[0:C]
        for h in range(1, RWKV_HEADS):
            out = out + tot[h * C:(h + 1) * C]
        y_ref[bi] = out * g_ref[bi]


def _rwkv_scan(r, k, v, an, b, ld, g, ln_w, ln_b, r_k):
    B, T, W = r.shape
    C = RWKV_CHUNK
    assert T % C == 0 and RWKV_HEADS * C == W and C == HEAD_DIM and RWKV_HEADS >= 3
    blk = pl.BlockSpec((B, C, W), lambda i: (0, i, 0))
    vec = pl.BlockSpec((1, W), lambda i: (0, 0))
    return pl.pallas_call(
        _rwkv_scan_kernel,
        grid=(T // C,),
        in_specs=[blk] * 7 + [vec, vec, vec],
        out_specs=blk,
        out_shape=jax.ShapeDtypeStruct((B, T, W), F32),
        scratch_shapes=[pltpu.VMEM((B, W, W), F32)],
        compiler_params=_cparams(("arbitrary",)),
        name="rwkv_scan",
    )(r, k, v, an, b, ld, g, ln_w, ln_b, r_k)


def _s5_kernel(u_ref, m1_ref, w1_ref, w2_ref, a1_ref, a2_ref, y_ref, *, chunks_per_seq):
    u = u_ref[0].astype(BF16)
    yi = jnp.dot(u, m1_ref[0], preferred_element_type=F32)
    x = jnp.dot(u, w1_ref[0], preferred_element_type=F32)
    crow = _iota(x.shape, 0) & (chunks_per_seq - 1)
    a1 = a1_ref[0]
    a2 = a2_ref[0]
    for s in range(int(math.log2(chunks_per_seq))):
        sh = 1 << s
        xs = jnp.where(crow >= sh, pltpu.roll(x, sh, axis=0), 0.0)
        x = x + a1[s:s + 1, :] * xs + a2[s:s + 1, :] * pltpu.roll(xs, S5_STATE, axis=1)
    s0 = jnp.where(crow >= 1, pltpu.roll(x, 1, axis=0), 0.0)
    y_ref[0] = yi + _bdot(s0, w2_ref[0])


def _s5_conv(u_t, m1, w1, w2, a1, a2, chunks_per_seq):
    G, R, K = u_t.shape
    assert chunks_per_seq & (chunks_per_seq - 1) == 0
    spec = lambda a: pl.BlockSpec((1,) + a.shape[1:], lambda g: (g, 0, 0))
    return pl.pallas_call(
        functools.partial(_s5_kernel, chunks_per_seq=chunks_per_seq),
        grid=(G,),
        in_specs=[spec(u_t), spec(m1), spec(w1), spec(w2), spec(a1), spec(a2)],
        out_specs=spec(u_t),
        out_shape=jax.ShapeDtypeStruct((G, R, K), F32),
        compiler_params=_cparams(("parallel",)),
        name="s5_conv",
    )(u_t, m1, w1, w2, a1, a2)


def _s5_weights(lam_re, lam_im, log_dt, b_re, b_im, c_re, c_im, chunks_per_seq):
    hp = lax.Precision.HIGHEST
    L = S5_CHUNK
    dt = jnp.exp(log_dt)[:, None]
    lr = lam_re * dt
    li = lam_im * dt
    mag = jnp.exp(lr)
    ab_re = mag * jnp.cos(li)
    ab_im = mag * jnp.sin(li)
    den = lam_re * lam_re + lam_im * lam_im
    nr = ab_re - 1.0
    coef_re = (nr * lam_re + ab_im * lam_im) / den
    coef_im = (ab_im * lam_re - nr * lam_im) / den
    bb_re = coef_re[..., None] * b_re - coef_im[..., None] * b_im
    bb_im = coef_re[..., None] * b_im + coef_im[..., None] * b_re

    def power(n):
        n = n.astype(F32)[:, None, None]
        m = jnp.exp(n * lr[None])
        return m * jnp.cos(n * li[None]), m * jnp.sin(n * li[None])

    pw_re, pw_im = power(jnp.arange(L + 1))
    cp_re = c_re[None] * pw_re[:, :, None, :] - c_im[None] * pw_im[:, :, None, :]
    cp_im = c_re[None] * pw_im[:, :, None, :] + c_im[None] * pw_re[:, :, None, :]
    kern = (jnp.einsum('nghp,gpk->nghk', cp_re[:L], bb_re, precision=hp)
            - jnp.einsum('nghp,gpk->nghk', cp_im[:L], bb_im, precision=hp))
    s_idx = jnp.arange(L)[:, None]
    t_idx = jnp.arange(L)[None, :]
    lag = t_idx - s_idx
    kt = kern[jnp.clip(lag, 0, L - 1)]
    kt = jnp.where((lag >= 0)[:, :, None, None, None], kt, 0.0)
    G = lam_re.shape[0]
    H = S5_GROUP_CH
    m1 = kt.transpose(2, 0, 4, 1, 3).reshape(G, L * H, L * H)
    qr = pw_re[L - 1 - jnp.arange(L)]
    qi = pw_im[L - 1 - jnp.arange(L)]
    w1_re = qr[..., None] * bb_re[None] - qi[..., None] * bb_im[None]
    w1_im = qr[..., None] * bb_im[None] + qi[..., None] * bb_re[None]
    w1 = jnp.concatenate([w1_re, w1_im], axis=2)
    w1 = w1.transpose(1, 0, 3, 2).reshape(G, L * H, 2 * S5_STATE)
    w2 = jnp.concatenate([cp_re[1:], -cp_im[1:]], axis=3)
    w2 = w2.transpose(1, 3, 0, 2).reshape(G, 2 * S5_STATE, L * H)
    nsteps = int(math.log2(chunks_per_seq))
    sr, si = power(L * (2 ** jnp.arange(nsteps)))
    a1 = jnp.concatenate([sr, sr], axis=2).transpose(1, 0, 2)
    a2 = jnp.concatenate([-si, si], axis=2).transpose(1, 0, 2)
    pad = (-nsteps) % 8
    a1 = jnp.pad(a1, ((0, 0), (0, pad), (0, 0)))
    a2 = jnp.pad(a2, ((0, 0), (0, pad), (0, 0)))
    return m1.astype(BF16), w1.astype(BF16), w2.astype(BF16), a1, a2


def _fox_prep_kernel(qk_ref, v_ref, f_ref, qg_ref, kg_ref, bf_ref, q_out, k_out, vt_out, carry_ref):
    @pl.when(pl.program_id(1) == 0)
    def _():
        carry_ref[...] = jnp.zeros_like(carry_ref)

    tb = f_ref.shape[1]
    vt = v_ref[0].T
    ones_row = (_iota((LANES - HEAD_DIM, tb), 0) == 0).astype(F32)
    for h in range(FOX_HEADS):
        vt_out[0, h, 0] = jnp.concatenate(
            [vt[h * HEAD_DIM:(h + 1) * HEAD_DIM, :], ones_row], axis=0).astype(BF16)

    x = f_ref[0] + bf_ref[...]
    logf = jnp.minimum(x, 0.0) - jnp.log(1.0 + jnp.exp(-jnp.abs(x)))
    tri = (_iota((tb, tb), 1) <= _iota((tb, tb), 0)).astype(BF16)
    cum = _dot_exact_lhs(tri, logf) + carry_ref[...]
    carry_ref[...] = cum[tb - 1:tb, :]
    c2 = cum * LOG2E
    c_hi = c2.astype(BF16).astype(F32)
    r1 = c2 - c_hi
    c_mid = r1.astype(BF16).astype(F32)
    c_lo = (r1 - c_mid).astype(BF16).astype(F32)

    lane = _iota((tb, LANES), 1)
    D = HEAD_DIM
    scale = LOG2E / math.sqrt(D)
    for h in range(FOX_HEADS):
        hi = jnp.broadcast_to(c_hi[:, h:h + 1], (tb, LANES))
        mid = jnp.broadcast_to(c_mid[:, h:h + 1], (tb, LANES))
        lo = jnp.broadcast_to(c_lo[:, h:h + 1], (tb, LANES))
        q = qk_ref[0, :, h * LANES:(h + 1) * LANES]
        k = qk_ref[0, :, (FOX_HEADS + h) * LANES:(FOX_HEADS + h + 1) * LANES]
        qn = q * lax.rsqrt(jnp.sum(q * q, axis=-1, keepdims=True) * (1.0 / D) + NORM_EPS) * qg_ref[...] * scale
        kn = k * lax.rsqrt(jnp.sum(k * k, axis=-1, keepdims=True) * (1.0 / D) + NORM_EPS) * kg_ref[...]
        q_aug = jnp.where(lane < D, qn,
                          jnp.where(lane < D + 3, 1.0,
                                    jnp.where(lane == D + 3, hi,
                                              jnp.where(lane == D + 4, mid,
                                                        jnp.where(lane == D + 5, lo, 0.0)))))
        k_aug = jnp.where(lane < D, kn,
                          jnp.where(lane == D, -hi,
                                    jnp.where(lane == D + 1, -mid,
                                              jnp.where(lane == D + 2, -lo,
                                                        jnp.where(lane < D + FOX_AUG, 1.0, 0.0)))))
        q_out[0, :, h * LANES:(h + 1) * LANES] = q_aug.astype(BF16)
        k_out[0, :, h * LANES:(h + 1) * LANES] = k_aug.astype(BF16)


def _fox_prep(qk, v, f, q_g, k_g, b_f, tb):
    B, T, _ = qk.shape
    HW = FOX_HEADS * LANES
    vec = pl.BlockSpec((1, LANES), lambda b, i: (0, 0))
    out_spec = pl.BlockSpec((1, tb, HW), lambda b, i: (b, i, 0))
    return pl.pallas_call(
        _fox_prep_kernel,
        grid=(B, T // tb),
        in_specs=[pl.BlockSpec((1, tb, 2 * HW), lambda b, i: (b, i, 0)),
                  pl.BlockSpec((1, tb, FOX_W), lambda b, i: (b, i, 0)),
                  pl.BlockSpec((1, tb, LANES), lambda b, i: (b, i, 0)),
                  vec, vec, vec],
        out_specs=[out_spec, out_spec,
                   pl.BlockSpec((1, FOX_HEADS, 1, LANES, tb), lambda b, i: (b, 0, i, 0, 0))],
        out_shape=[jax.ShapeDtypeStruct((B, T, HW), BF16)] * 2
        + [jax.ShapeDtypeStruct((B, FOX_HEADS, T // tb, LANES, tb), BF16)],
        scratch_shapes=[pltpu.VMEM((1, LANES), F32)],
        compiler_params=_cparams(("arbitrary", "arbitrary")),
        name="fox_prep",
    )(qk, v, f, q_g, k_g, b_f)


def _fox_attn_kernel(q_ref, k_ref, vt_ref, o_ref, m_ref, acc_ref, sa_ref, sb_ref, *, tq):
    qi = pl.program_id(2)
    m_ref[...] = jnp.full(m_ref.shape, NEG_BIG, F32)
    acc_ref[...] = jnp.zeros_like(acc_ref)

    tqh = tq // 2

    chains = [(hh, half) for hh in range(2) for half in range(2)]

    def scores(j, s_ref):
        start = pl.multiple_of(j * tq, tq)
        for c, (hh, half) in enumerate(chains):
            hs = slice(hh * LANES, (hh + 1) * LANES)
            qs = slice(half * tqh, (half + 1) * tqh)
            s_ref[c] = lax.dot_general(k_ref[0, pl.ds(start, tq), hs], q_ref[0, qs, hs],
                                       (((1,), (1,)), ((), ())),
                                       preferred_element_type=F32)

    def softmax_pv(j, s_ref, masked):
        for c, (hh, half) in enumerate(chains):
            st = s_ref[c]
            qs = slice(half * tqh, (half + 1) * tqh)
            if masked:
                keep = _iota(st.shape, 0) <= _iota(st.shape, 1) + half * tqh
                st = jnp.where(keep, st, NEG_BIG)
            m_old = m_ref[hh, :, qs]
            m_new = jnp.maximum(m_old, jnp.max(st, axis=0, keepdims=True))
            p = jnp.exp2(st - m_new).astype(BF16)
            alpha = jnp.exp2(m_old - m_new)
            m_ref[hh, :, qs] = m_new
            acc_ref[hh, :, qs] = alpha * acc_ref[hh, :, qs] + jnp.dot(vt_ref[0, hh, j], p,
                                                                       preferred_element_type=F32)

    def body(i, carry):
        j = 2 * i
        scores(j + 1, sb_ref)
        softmax_pv(j, sa_ref, False)
        scores(j + 2, sa_ref)
        softmax_pv(j + 1, sb_ref, False)
        return carry

    scores(0, sa_ref)
    lax.fori_loop(0, qi // 2, body, 0)

    @pl.when(qi % 2 == 0)
    def _():
        softmax_pv(qi, sa_ref, True)

    @pl.when(qi % 2 == 1)
    def _():
        scores(qi, sb_ref)
        softmax_pv(qi - 1, sa_ref, False)
        softmax_pv(qi, sb_ref, True)

    outs = []
    for hh in range(2):
        acc = acc_ref[hh]
        outs.append(acc[:HEAD_DIM, :] / acc[HEAD_DIM:HEAD_DIM + 1, :])
    o_ref[0] = jnp.concatenate(outs, axis=0).T


def _fox_attn(q_aug, k_aug, vt, tq):
    B, T, _ = q_aug.shape
    npairs = FOX_HEADS // 2
    return pl.pallas_call(
        functools.partial(_fox_attn_kernel, tq=tq),
        grid=(B, npairs, T // tq),
        in_specs=[pl.BlockSpec((1, tq, 2 * LANES), lambda b, h, i: (b, i, h)),
                  pl.BlockSpec((1, T, 2 * LANES), lambda b, h, i: (b, 0, h)),
                  pl.BlockSpec((1, 2, T // tq, LANES, tq), lambda b, h, i: (b, h, 0, 0, 0))],
        out_specs=pl.BlockSpec((1, tq, LANES), lambda b, h, i: (b, i, h)),
        out_shape=jax.ShapeDtypeStruct((B, T, FOX_W), F32),
        scratch_shapes=[pltpu.VMEM((2, 1, tq), F32), pltpu.VMEM((2, LANES, tq), F32),
                        pltpu.VMEM((4, tq, tq // 2), F32), pltpu.VMEM((4, tq, tq // 2), F32)],
        compiler_params=_cparams(("parallel", "parallel", "arbitrary")),
        name="fox_attn",
    )(q_aug, k_aug, vt)


def _gelu_tanh(x):
    return 0.5 * x * (1.0 + jnp.tanh(math.sqrt(2.0 / math.pi) * (x + 0.044715 * (x * x * x))))


def _mix_out_kernel(h_ref, yr_ref, ys_ref, us_ref, of_ref, og_ref, mk_ref, mv_ref,
                    d_ref, bglu_ref, sg_ref, fg_ref, xg_ref,
                    wglu_ref, wout_ref, wxq_ref, wxo_ref, o_ref):
    h = h_ref[...]
    ys = _gelu_tanh(ys_ref[...] + d_ref[...] * us_ref[...])
    ys = ys * _sigmoid(_bdot(ys, wglu_ref[...]) + bglu_ref[...])
    ys = _rms(ys, sg_ref[...])
    yf = _rms(of_ref[...] * _sigmoid(og_ref[...]), fg_ref[...])
    r0, r1 = RWKV_W, RWKV_W + S5_W
    h = h + (_bdot(yr_ref[...], wout_ref[0:r0, :]) + _bdot(ys, wout_ref[r0:r1, :])
             + _bdot(yf, wout_ref[r1:, :]))
    q = jnp.dot(_rms(h, xg_ref[...]).astype(BF16), wxq_ref[...], preferred_element_type=F32)
    scale = 1.0 / math.sqrt(XATTN_HEAD_DIM)
    upd = jnp.zeros_like(h)
    for hd in range(XATTN_HEADS):
        cs = slice(hd * XATTN_HEAD_DIM, (hd + 1) * XATTN_HEAD_DIM)
        s = _bdot_nt(q[:, cs], mk_ref[0, :, cs]) * scale
        s = s - jnp.max(s, axis=-1, keepdims=True)
        p = jnp.exp(s)
        p = p / jnp.sum(p, axis=-1, keepdims=True)
        o = _bdot(p, mv_ref[0, :, cs])
        upd = upd + _bdot(o, wxo_ref[cs, :])
    o_ref[...] = h + upd


def _mix_out(h, yr, ys, us, of, og, mem_k, mem_v, d, bglu, sg, fg, xg, wglu, wout, wxq, wxo,
             rows_per_batch, tm=512):
    n, dm = h.shape
    tm = min(tm, rows_per_batch)
    assert rows_per_batch % tm == 0
    bpb = rows_per_batch // tm
    rowblk = lambda w: pl.BlockSpec((tm, w), lambda i: (i, 0))
    vec = lambda w: pl.BlockSpec((1, w), lambda i: (0, 0))
    memblk = pl.BlockSpec((1,) + mem_k.shape[1:], lambda i: (i // bpb, 0, 0))
    return pl.pallas_call(
        _mix_out_kernel,
        grid=(n // tm,),
        in_specs=[rowblk(dm), rowblk(RWKV_W), rowblk(S5_W), rowblk(S5_W), rowblk(FOX_W), rowblk(FOX_W),
                  memblk, memblk,
                  vec(S5_W), vec(S5_W), vec(S5_W), vec(FOX_W), vec(dm),
                  _vmem_full(), _vmem_full(), _vmem_full(), _vmem_full()],
        out_specs=rowblk(dm),
        out_shape=jax.ShapeDtypeStruct((n, dm), F32),
        compiler_params=_cparams(("parallel",)),
        name="mix_out",
    )(h, yr, ys, us, of, og, mem_k, mem_v, d, bglu, sg, fg, xg, wglu, wout, wxq, wxo)


def _ffn_kernel(h_ref, g_ref, w1_ref, w2_ref, fg_ref, o_ref, *, ff_chunk, final_norm):
    h = h_ref[...]
    hn = _rms(h, g_ref[...]).astype(BF16)
    acc = h
    for c0 in range(0, w1_ref.shape[1], ff_chunk):
        z = jnp.dot(hn, w1_ref[:, c0:c0 + ff_chunk], preferred_element_type=F32)
        z = jnp.maximum(z, 0.0)
        acc = acc + jnp.dot((z * z).astype(BF16), w2_ref[c0:c0 + ff_chunk, :], preferred_element_type=F32)
    if final_norm:
        acc = _rms(acc, fg_ref[...])
    o_ref[...] = acc


def _ffn(h, g, w1, w2, fg, final_norm, tm=512):
    n, dm = h.shape
    tm = min(tm, n)
    assert n % tm == 0
    vec = pl.BlockSpec((1, dm), lambda i: (0, 0))
    return pl.pallas_call(
        functools.partial(_ffn_kernel, ff_chunk=1024, final_norm=final_norm),
        grid=(n // tm,),
        in_specs=[pl.BlockSpec((tm, dm), lambda i: (i, 0)), vec, _vmem_full(), _vmem_full(), vec],
        out_specs=pl.BlockSpec((tm, dm), lambda i: (i, 0)),
        out_shape=jax.ShapeDtypeStruct((n, dm), F32),
        compiler_params=_cparams(("parallel",)),
        name="ffn",
    )(h, g.reshape(1, dm), w1, w2, fg.reshape(1, dm))


def _head_block_ones(width):
    i = jnp.arange(width) // HEAD_DIM
    return (i[:, None] == i[None, :]).astype(BF16)


def _rwkv_reorder(x):
    W = RWKV_W
    o_k = W + DECAY_LORA
    o_v = o_k + W
    o_a = o_v + W
    return jnp.concatenate([x[..., :W], x[..., o_k:o_v], x[..., o_v:o_a], x[..., W:o_k], x[..., o_a:]], axis=-1)


def _pad_heads(w, nheads):
    d = w.shape[0]
    w = w.reshape(d, nheads, HEAD_DIM)
    w = jnp.pad(w, ((0, 0), (0, 0), (0, LANES - HEAD_DIM)))
    return w.reshape(d, nheads * LANES)


def _pad_lanes(v, width=LANES):
    return jnp.pad(v, (0, width - v.shape[0])).reshape(1, width)


_QK_W = 2 * FOX_HEADS * LANES
_COLS = {}
_c = 0
for _name, _w in (("rwkv", RWKV_IN), ("s5", S5_W), ("qk", _QK_W), ("v", FOX_W), ("og", FOX_W), ("f", LANES)):
    _COLS[_name] = (_c, _c + _w)
    _c += _w
_COL_ORDER = ("rwkv", "s5", "qk", "v", "og", "f")


def _layout_w_in(w_in):
    o = RWKV_IN + S5_W
    w_r = _rwkv_reorder(w_in[:, :RWKV_IN])
    w_s = w_in[:, RWKV_IN:o]
    w_q = _pad_heads(w_in[:, o:o + FOX_W], FOX_HEADS)
    w_k = _pad_heads(w_in[:, o + FOX_W:o + 2 * FOX_W], FOX_HEADS)
    w_v = w_in[:, o + 2 * FOX_W:o + 3 * FOX_W]
    w_g = w_in[:, o + 3 * FOX_W:o + 4 * FOX_W]
    w_f = jnp.pad(w_in[:, o + 4 * FOX_W:], ((0, 0), (0, LANES - FOX_HEADS)))
    return jnp.concatenate([w_r, w_s, w_q, w_k, w_v, w_g, w_f], axis=1).astype(BF16)


def kernel(x, mem, mem_norm_g, w_mem_kv, mix_norm_g, w_in, rwkv_mu, rwkv_w0, rwkv_w2, rwkv_a0, rwkv_a2, rwkv_g2, rwkv_k_k, rwkv_k_a, rwkv_r_k, rwkv_ln_w, rwkv_ln_b, s5_lam_re, s5_lam_im, s5_log_dt, s5_b_re, s5_b_im, s5_c_re, s5_c_im, s5_d, s5_w_glu, s5_b_glu, s5_out_g, fox_q_g, fox_k_g, fox_b_f, fox_out_g, w_out, xattn_norm_g, w_xq, w_xo, ffn_norm_g, w_ffn1, w_ffn2, final_norm_g):
    B, T, D = x.shape
    M = mem.shape[1]
    N = B * T
    depth = w_in.shape[0]
    W = RWKV_W
    nchunks = T // S5_CHUNK
    e_head = _head_block_ones(W)

    mem_kv = _norm_matmul(mem.reshape(B * M, D), mem_norm_g, w_mem_kv.astype(BF16),
                          [(0, D), (D, 2 * D)], [BF16, BF16], tm=256)
    mem_k = mem_kv[0].reshape(B, M, D)
    mem_v = mem_kv[1].reshape(B, M, D)

    col_ranges = [_COLS[n] for n in _COL_ORDER]
    col_dtypes = [F32] * len(col_ranges)
    h = x.reshape(N, D)
    for l in range(depth):
        p_r, p_s, p_qk, p_v, p_og, p_f = _norm_matmul(
            h, mix_norm_g[l], _layout_w_in(w_in[l]), col_ranges, col_dtypes)

        wwa = jnp.zeros((LANES, 2 * W), F32)
        wwa = wwa.at[:DECAY_LORA, :W].set(rwkv_w2[l]).at[DECAY_LORA:, W:].set(rwkv_a2[l]).astype(BF16)
        row = lambda v: v.reshape(1, -1)
        r, k, v, an, b, ld, g = _rwkv_prep(
            p_r.reshape(B, T, RWKV_IN), row(_rwkv_reorder(rwkv_mu[l])), row(rwkv_w0[l]), row(rwkv_a0[l]),
            row(rwkv_k_k[l]), row(rwkv_k_a[l]), wwa, rwkv_g2[l].astype(BF16), e_head)
        y_r = _rwkv_scan(r, k, v, an, b, ld, g, row(rwkv_ln_w[l]), row(rwkv_ln_b[l]),
                         row(rwkv_r_k[l])).reshape(N, W)

        m1, w1, w2, a1, a2 = _s5_weights(s5_lam_re[l], s5_lam_im[l], s5_log_dt[l], s5_b_re[l], s5_b_im[l],
                                         s5_c_re[l], s5_c_im[l], nchunks)
        u_t = p_s.reshape(B * nchunks, S5_CHUNK, S5_GROUPS, S5_GROUP_CH).transpose(2, 0, 1, 3)
        u_t = u_t.reshape(S5_GROUPS, B * nchunks, S5_CHUNK * S5_GROUP_CH)
        y_t = _s5_conv(u_t, m1, w1, w2, a1, a2, nchunks)
        y_s = y_t.reshape(S5_GROUPS, B * nchunks, S5_CHUNK, S5_GROUP_CH).transpose(1, 2, 0, 3).reshape(N, S5_W)

        fox_blk = min(FOX_BLOCK, T)
        q_aug, k_aug, v_t = _fox_prep(p_qk.reshape(B, T, _QK_W), p_v.reshape(B, T, FOX_W),
                                      p_f.reshape(B, T, LANES), _pad_lanes(fox_q_g[l]),
                                      _pad_lanes(fox_k_g[l]), _pad_lanes(fox_b_f[l]), fox_blk)
        o_f = _fox_attn(q_aug, k_aug, v_t, fox_blk).reshape(N, FOX_W)

        h = _mix_out(h, y_r, y_s, p_s, o_f, p_og, mem_k, mem_v,
                     row(s5_d[l]), row(s5_b_glu[l]), row(s5_out_g[l]), row(fox_out_g[l]), row(xattn_norm_g[l]),
                     s5_w_glu[l].astype(BF16), w_out[l].astype(BF16), w_xq[l].astype(BF16),
                     w_xo[l].astype(BF16), rows_per_batch=T)
        h = _ffn(h, ffn_norm_g[l], w_ffn1[l].astype(BF16), w_ffn2[l].astype(BF16), final_norm_g,
                 final_norm=(l == depth - 1))
    return h.reshape(B, T, D)
```

```python
import functools
import math

import jax
import jax.numpy as jnp
from jax import lax
from jax.experimental import pallas as pl
from jax.experimental.pallas import tpu as pltpu

F32 = jnp.float32
BF16 = jnp.bfloat16

D_MODEL = 1024
RWKV_HEADS = 4
HEAD_DIM = 64
RWKV_W = RWKV_HEADS * HEAD_DIM
DECAY_LORA = 64
AAA_LORA = 64
GATE_LORA = 128
RWKV_IN = 3 * RWKV_W + DECAY_LORA + AAA_LORA + GATE_LORA
S5_GROUPS = 16
S5_GROUP_CH = 16
S5_W = S5_GROUPS * S5_GROUP_CH
S5_STATE = 64
FOX_HEADS = 8
FOX_W = FOX_HEADS * HEAD_DIM
XATTN_HEADS = 4
XATTN_HEAD_DIM = D_MODEL // XATTN_HEADS
NORM_EPS = 1e-6
RWKV_LN_EPS = 64e-5

LANES = 128
RWKV_CHUNK = 64
S5_CHUNK = 16
FOX_AUG = 6
FOX_BLOCK = 512
VMEM_LIMIT = 56 * 1024 * 1024

NEG_BIG = -1e30
LOG2E = 1.4426950408889634


def _cparams(sem):
    return pltpu.CompilerParams(dimension_semantics=sem, vmem_limit_bytes=VMEM_LIMIT)


def _bdot(a, b):
    return jnp.dot(a.astype(BF16), b.astype(BF16), preferred_element_type=F32)


def _bdot_nt(a, b):
    return lax.dot_general(a.astype(BF16), b.astype(BF16), (((1,), (1,)), ((), ())),
                           preferred_element_type=F32)


def _split2(x):
    hi = x.astype(BF16)
    lo = (x - hi.astype(F32)).astype(BF16)
    return hi, lo


def _split3(x):
    hi = x.astype(BF16)
    r1 = x - hi.astype(F32)
    mid = r1.astype(BF16)
    lo = (r1 - mid.astype(F32)).astype(BF16)
    return hi, mid, lo


def _dot_exact_lhs(a_bf16, x):
    hi, mid, lo = _split3(x)
    d = lambda p: jnp.dot(a_bf16, p, preferred_element_type=F32)
    return d(hi) + d(mid) + d(lo)


def _dot_exact_rhs(x, b_bf16):
    hi, lo = _split2(x)
    d = lambda p: jnp.dot(p, b_bf16, preferred_element_type=F32)
    return d(hi) + d(lo)


def _mm3(a, b):
    ah, al = _split2(a)
    bh, bl = _split2(b)
    d = lambda p, q: jnp.dot(p, q, preferred_element_type=F32)
    return d(ah, bh) + d(ah, bl) + d(al, bh)


def _pack_heads(x):
    n = x.shape[0]
    d = HEAD_DIM
    hi = x.astype(BF16)
    lo_f = x - hi.astype(F32)
    lo = lo_f.astype(BF16)
    rows = lambda z, s: jnp.concatenate([z[n - s:], z[:n - s]], axis=0)
    lhs = (hi + pltpu.roll(lo_f, d, axis=1).astype(BF16)
           + jnp.concatenate([hi[:, n - 2 * d:], hi[:, :n - 2 * d]], axis=1))
    rhs = hi + rows(hi, d) + rows(lo, 2 * d)
    return lhs, rhs


def _mm_packed(lhs, rhs, head_mask):
    return jnp.where(head_mask, jnp.dot(lhs, rhs, preferred_element_type=F32), 0.0)


def _sigmoid(x):
    return 1.0 / (1.0 + jnp.exp(-x))


def _softplus(x):
    return jnp.maximum(x, 0.0) + jnp.log(1.0 + jnp.exp(-jnp.abs(x)))


def _rms(x, g, eps=NORM_EPS):
    return x * lax.rsqrt(jnp.mean(x * x, axis=-1, keepdims=True) + eps) * g


def _iota(shape, dim):
    return lax.broadcasted_iota(jnp.int32, shape, dim)


def _vmem_full():
    return pl.BlockSpec(memory_space=pltpu.VMEM)


def _norm_matmul_kernel(x_ref, g_ref, w_ref, *out_refs, col_ranges, col_chunk):
    xn = _rms(x_ref[...], g_ref[...]).astype(BF16)
    for o_ref, (c0, c1) in zip(out_refs, col_ranges):
        for cc in range(c0, c1, col_chunk):
            ce = min(cc + col_chunk, c1)
            o_ref[:, cc - c0:ce - c0] = jnp.dot(
                xn, w_ref[:, cc:ce], preferred_element_type=F32).astype(o_ref.dtype)


def _norm_matmul(x, g, w, col_ranges, out_dtypes, tm=512):
    n, d = x.shape
    tm = min(tm, n)
    assert n % tm == 0
    kern = functools.partial(_norm_matmul_kernel, col_ranges=tuple(col_ranges), col_chunk=512)
    return pl.pallas_call(
        kern,
        grid=(n // tm,),
        in_specs=[pl.BlockSpec((tm, d), lambda i: (i, 0)),
                  pl.BlockSpec((1, d), lambda i: (0, 0)),
                  _vmem_full()],
        out_specs=[pl.BlockSpec((tm, c1 - c0), lambda i: (i, 0)) for (c0, c1) in col_ranges],
        out_shape=[jax.ShapeDtypeStruct((n, c1 - c0), dt) for (c0, c1), dt in zip(col_ranges, out_dtypes)],
        compiler_params=_cparams(("parallel",)),
        name="norm_matmul",
    )(x, g.reshape(1, d), w)


def _rwkv_prep_kernel(p_ref, mu_ref, w0_ref, a0_ref, kk_ref, ka_ref, wwa_ref, g2_ref, e_ref,
                      r_out, k_out, v_out, an_out, b_out, ld_out, g_out, carry_ref):
    W = RWKV_W

    @pl.when(pl.program_id(1) == 0)
    def _():
        carry_ref[...] = jnp.zeros_like(carry_ref)

    p = p_ref[0]
    tb = p.shape[0]
    prev = carry_ref[...]
    rolled = pltpu.roll(p, 1, axis=0)
    shifted = jnp.where(_iota(p.shape, 0) == 0, prev, rolled)
    carry_ref[...] = p[tb - 1:tb, :]
    p = p + (shifted - p) * mu_ref[...]

    r = p[:, 0:W]
    k = p[:, W:2 * W]
    v = p[:, 2 * W:3 * W]
    wa_l = p[:, 3 * W:3 * W + LANES]
    g_l = p[:, 3 * W + LANES:]

    lora_in = jnp.where(_iota(wa_l.shape, 1) < DECAY_LORA, jnp.tanh(wa_l), wa_l)
    lora = _bdot(lora_in, wwa_ref[...])
    w = -_softplus(-(w0_ref[...] + lora[:, :W])) - 0.5
    a = _sigmoid(a0_ref[...] + lora[:, W:])
    g = _bdot(_sigmoid(g_l), g2_ref[...])

    kk = k * kk_ref[...]
    ss = _dot_exact_rhs(kk * kk, e_ref[...])
    kk = kk / jnp.maximum(jnp.sqrt(ss), 1e-12)
    k = k * (1.0 + (a - 1.0) * ka_ref[...])

    r_out[0] = r
    k_out[0] = k
    v_out[0] = v
    an_out[0] = -kk
    b_out[0] = kk * a
    ld_out[0] = -jnp.exp(w)
    g_out[0] = g


def _rwkv_prep(p, mu, w0, a0, k_k, k_a, wwa, g2, e_head, tb=256):
    B, T, C = p.shape
    tb = min(tb, T)
    W = RWKV_W
    vec = lambda n: pl.BlockSpec((1, n), lambda b, i: (0, 0))
    full = lambda s: pl.BlockSpec(s, lambda b, i: (0,) * len(s))
    out_spec = pl.BlockSpec((1, tb, W), lambda b, i: (b, i, 0))
    return pl.pallas_call(
        _rwkv_prep_kernel,
        grid=(B, T // tb),
        in_specs=[pl.BlockSpec((1, tb, C), lambda b, i: (b, i, 0)),
                  vec(C), vec(W), vec(W), vec(W), vec(W),
                  full(wwa.shape), full(g2.shape), full(e_head.shape)],
        out_specs=[out_spec] * 7,
        out_shape=[jax.ShapeDtypeStruct((B, T, W), F32)] * 7,
        scratch_shapes=[pltpu.VMEM((1, C), F32)],
        compiler_params=_cparams(("arbitrary", "arbitrary")),
        name="rwkv_prep",
    )(p, mu, w0, a0, k_k, k_a, wwa, g2, e_head)


def _stack_heads(x, head_mask):
    return jnp.where(head_mask, jnp.concatenate([x] * RWKV_HEADS, axis=0), 0.0)


def _rwkv_scan_kernel(r_ref, k_ref, v_ref, an_ref, b_ref, ld_ref, g_ref,
                      lnw_ref, lnb_ref, rk_ref, y_ref, state_ref):
    C = RWKV_CHUNK
    W = RWKV_W
    nb = r_ref.shape[0]

    @pl.when(pl.program_id(0) == 0)
    def _():
        state_ref[...] = jnp.zeros_like(state_ref)

    row = _iota((W, W), 0)
    col = _iota((W, W), 1)
    strict = col < row
    incl = col <= row
    eye = (col == row).astype(F32)
    head_mask = (row // C) == (col // HEAD_DIM)
    tri = (_iota((C, C), 1) <= _iota((C, C), 0)).astype(BF16)
    stack = lambda z: _stack_heads(z, head_mask)
    mm_packed = lambda p, q: _mm_packed(p, q, head_mask)

    bs = range(nb)
    each = lambda f, *cols: [f(*xs) for xs in zip(*cols)]

    cum = [_dot_exact_lhs(tri, ld_ref[bi]) for bi in bs]
    p_inc = each(jnp.exp, cum)
    p_prev = [jnp.exp(cum[bi] - ld_ref[bi]) for bi in bs]
    p_inv = each(lambda c: jnp.exp(-c), cum)
    p_end = each(lambda p: p[C - 1:C, :], p_inc)

    rt = [stack(r_ref[bi] * p_inc[bi]) for bi in bs]
    at = [stack(an_ref[bi] * p_prev[bi]) for bi in bs]
    bt = [b_ref[bi] * p_inv[bi] for bi in bs]
    kt = [k_ref[bi] * p_inv[bi] for bi in bs]
    vs = [stack(v_ref[bi]) for bi in bs]
    bk = [jnp.concatenate([stack(bt[bi] * p_end[bi]), stack(kt[bi] * p_end[bi])], axis=0) for bi in bs]

    aa = [_bdot_nt(jnp.concatenate([at[bi], rt[bi]], axis=0),
                   jnp.concatenate([stack(bt[bi]), stack(kt[bi])], axis=0)) for bi in bs]
    n_ab = each(lambda z: jnp.where(strict, z[:W, :W], 0.0), aa)
    a_ak = each(lambda z: jnp.where(strict, z[:W, W:], 0.0), aa)
    a_rb = each(lambda z: jnp.where(incl, z[W:, :W], 0.0), aa)
    a_rk = each(lambda z: jnp.where(incl, z[W:, W:], 0.0), aa)

    inv = each(lambda n: eye + n, n_ab)
    pw_l, pw_r = zip(*each(_pack_heads, n_ab))
    for _ in range(int(math.log2(C)) - 1):
        pw = each(mm_packed, pw_l, pw_r)
        pw_l, pw_r = zip(*each(_pack_heads, pw))
        inv = each(lambda i, pl_: i + mm_packed(pl_, _pack_heads(i)[1]), inv, pw_l)

    av = each(_bdot, a_ak, vs)
    inv_l = each(lambda i: _pack_heads(i)[0], inv)
    w_t = each(lambda il, z: mm_packed(il, _pack_heads(z)[1]), inv_l, at)
    u_t = each(lambda il, z: mm_packed(il, _pack_heads(z)[1]), inv_l, av)
    y_t = each(_bdot, a_rk, vs)

    H = [state_ref[bi] for bi in bs]
    x = [_bdot(jnp.concatenate([w_t[bi], rt[bi]], axis=0), H[bi]) for bi in bs]
    u = [x[bi][:W] + u_t[bi] for bi in bs]
    y = [x[bi][W:] + y_t[bi] + _bdot(a_rb[bi], u[bi]) for bi in bs]

    for bi in bs:
        uv = jnp.concatenate([u[bi], vs[bi]], axis=0)
        p_col = jnp.broadcast_to(p_end[bi], (8, W)).T[:, 0:1]
        state_ref[bi] = H[bi] * p_col + _bdot(bk[bi].T, uv)

    inv_n = 1.0 / HEAD_DIM
    for bi in bs:
        mean = jnp.sum(y[bi], axis=-1, keepdims=True) * inv_n
        yc = jnp.where(head_mask, y[bi] - mean, 0.0)
        var = jnp.sum(yc * yc, axis=-1, keepdims=True) * inv_n
        yn = jnp.where(head_mask, yc * lax.rsqrt(var + RWKV_LN_EPS) * lnw_ref[...] + lnb_ref[...], 0.0)
        bonus = jnp.sum(stack(r_ref[bi] * k_ref[bi] * rk_ref[...]), axis=-1, keepdims=True)
        tot = yn + bonus * vs[bi]
        out = tot[0:C]
        for h in range(1, RWKV_HEADS):
            out = out + tot[h * C:(h + 1) * C]
        y_ref[bi] = out * g_ref[bi]


def _rwkv_scan(r, k, v, an, b, ld, g, ln_w, ln_b, r_k):
    B, T, W = r.shape
    C = RWKV_CHUNK
    assert T % C == 0 and RWKV_HEADS * C == W and C == HEAD_DIM and RWKV_HEADS >= 3
    blk = pl.BlockSpec((B, C, W), lambda i: (0, i, 0))
    vec = pl.BlockSpec((1, W), lambda i: (0, 0))
    return pl.pallas_call(
        _rwkv_scan_kernel,
        grid=(T // C,),
        in_specs=[blk] * 7 + [vec, vec, vec],
        out_specs=blk,
        out_shape=jax.ShapeDtypeStruct((B, T, W), F32),
        scratch_shapes=[pltpu.VMEM((B, W, W), F32)],
        compiler_params=_cparams(("arbitrary",)),
        name="rwkv_scan",
    )(r, k, v, an, b, ld, g, ln_w, ln_b, r_k)


def _s5_kernel(u_ref, m1_ref, w1_ref, w2_ref, a1_ref, a2_ref, y_ref, *, chunks_per_seq):
    u = u_ref[0].astype(BF16)
    yi = jnp.dot(u, m1_ref[0], preferred_element_type=F32)
    x = jnp.dot(u, w1_ref[0], preferred_element_type=F32)
    crow = _iota(x.shape, 0) & (chunks_per_seq - 1)
    a1 = a1_ref[0]
    a2 = a2_ref[0]
    for s in range(int(math.log2(chunks_per_seq))):
        sh = 1 << s
        xs = jnp.where(crow >= sh, pltpu.roll(x, sh, axis=0), 0.0)
        x = x + a1[s:s + 1, :] * xs + a2[s:s + 1, :] * pltpu.roll(xs, S5_STATE, axis=1)
    s0 = jnp.where(crow >= 1, pltpu.roll(x, 1, axis=0), 0.0)
    y_ref[0] = yi + _bdot(s0, w2_ref[0])


def _s5_conv(u_t, m1, w1, w2, a1, a2, chunks_per_seq):
    G, R, K = u_t.shape
    assert chunks_per_seq & (chunks_per_seq - 1) == 0
    spec = lambda a: pl.BlockSpec((1,) + a.shape[1:], lambda g: (g, 0, 0))
    return pl.pallas_call(
        functools.partial(_s5_kernel, chunks_per_seq=chunks_per_seq),
        grid=(G,),
        in_specs=[spec(u_t), spec(m1), spec(w1), spec(w2), spec(a1), spec(a2)],
        out_specs=spec(u_t),
        out_shape=jax.ShapeDtypeStruct((G, R, K), F32),
        compiler_params=_cparams(("parallel",)),
        name="s5_conv",
    )(u_t, m1, w1, w2, a1, a2)


def _s5_weights(lam_re, lam_im, log_dt, b_re, b_im, c_re, c_im, chunks_per_seq):
    hp = lax.Precision.HIGHEST
    L = S5_CHUNK
    dt = jnp.exp(log_dt)[:, None]
    lr = lam_re * dt
    li = lam_im * dt
    mag = jnp.exp(lr)
    ab_re = mag * jnp.cos(li)
    ab_im = mag * jnp.sin(li)
    den = lam_re * lam_re + lam_im * lam_im
    nr = ab_re - 1.0
    coef_re = (nr * lam_re + ab_im * lam_im) / den
    coef_im = (ab_im * lam_re - nr * lam_im) / den
    bb_re = coef_re[..., None] * b_re - coef_im[..., None] * b_im
    bb_im = coef_re[..., None] * b_im + coef_im[..., None] * b_re

    def power(n):
        n = n.astype(F32)[:, None, None]
        m = jnp.exp(n * lr[None])
        return m * jnp.cos(n * li[None]), m * jnp.sin(n * li[None])

    pw_re, pw_im = power(jnp.arange(L + 1))
    cp_re = c_re[None] * pw_re[:, :, None, :] - c_im[None] * pw_im[:, :, None, :]
    cp_im = c_re[None] * pw_im[:, :, None, :] + c_im[None] * pw_re[:, :, None, :]
    kern = (jnp.einsum('nghp,gpk->nghk', cp_re[:L], bb_re, precision=hp)
            - jnp.einsum('nghp,gpk->nghk', cp_im[:L], bb_im, precision=hp))
    s_idx = jnp.arange(L)[:, None]
    t_idx = jnp.arange(L)[None, :]
    lag = t_idx - s_idx
    kt = kern[jnp.clip(lag, 0, L - 1)]
    kt = jnp.where((lag >= 0)[:, :, None, None, None], kt, 0.0)
    G = lam_re.shape[0]
    H = S5_GROUP_CH
    m1 = kt.transpose(2, 0, 4, 1, 3).reshape(G, L * H, L * H)
    qr = pw_re[L - 1 - jnp.arange(L)]
    qi = pw_im[L - 1 - jnp.arange(L)]
    w1_re = qr[..., None] * bb_re[None] - qi[..., None] * bb_im[None]
    w1_im = qr[..., None] * bb_im[None] + qi[..., None] * bb_re[None]
    w1 = jnp.concatenate([w1_re, w1_im], axis=2)
    w1 = w1.transpose(1, 0, 3, 2).reshape(G, L * H, 2 * S5_STATE)
    w2 = jnp.concatenate([cp_re[1:], -cp_im[1:]], axis=3)
    w2 = w2.transpose(1, 3, 0, 2).reshape(G, 2 * S5_STATE, L * H)
    nsteps = int(math.log2(chunks_per_seq))
    sr, si = power(L * (2 ** jnp.arange(nsteps)))
    a1 = jnp.concatenate([sr, sr], axis=2).transpose(1, 0, 2)
    a2 = jnp.concatenate([-si, si], axis=2).transpose(1, 0, 2)
    pad = (-nsteps) % 8
    a1 = jnp.pad(a1, ((0, 0), (0, pad), (0, 0)))
    a2 = jnp.pad(a2, ((0, 0), (0, pad), (0, 0)))
    return m1.astype(BF16), w1.astype(BF16), w2.astype(BF16), a1, a2


def _fox_prep_kernel(qk_ref, v_ref, f_ref, qg_ref, kg_ref, bf_ref, q_out, k_out, vt_out, carry_ref):
    @pl.when(pl.program_id(1) == 0)
    def _():
        carry_ref[...] = jnp.zeros_like(carry_ref)

    tb = f_ref.shape[1]
    vt = v_ref[0].T
    ones_row = (_iota((LANES - HEAD_DIM, tb), 0) == 0).astype(F32)
    for h in range(FOX_HEADS):
        vt_out[0, h, 0] = jnp.concatenate(
            [vt[h * HEAD_DIM:(h + 1) * HEAD_DIM, :], ones_row], axis=0).astype(BF16)

    x = f_ref[0] + bf_ref[...]
    logf = jnp.minimum(x, 0.0) - jnp.log(1.0 + jnp.exp(-jnp.abs(x)))
    tri = (_iota((tb, tb), 1) <= _iota((tb, tb), 0)).astype(BF16)
    cum = _dot_exact_lhs(tri, logf) + carry_ref[...]
    carry_ref[...] = cum[tb - 1:tb, :]
    c2 = cum * LOG2E
    c_hi = c2.astype(BF16).astype(F32)
    r1 = c2 - c_hi
    c_mid = r1.astype(BF16).astype(F32)
    c_lo = (r1 - c_mid).astype(BF16).astype(F32)

    lane = _iota((tb, LANES), 1)
    D = HEAD_DIM
    scale = LOG2E / math.sqrt(D)
    for h in range(FOX_HEADS):
        hi = jnp.broadcast_to(c_hi[:, h:h + 1], (tb, LANES))
        mid = jnp.broadcast_to(c_mid[:, h:h + 1], (tb, LANES))
        lo = jnp.broadcast_to(c_lo[:, h:h + 1], (tb, LANES))
        q = qk_ref[0, :, h * LANES:(h + 1) * LANES]
        k = qk_ref[0, :, (FOX_HEADS + h) * LANES:(FOX_HEADS + h + 1) * LANES]
        qn = q * lax.rsqrt(jnp.sum(q * q, axis=-1, keepdims=True) * (1.0 / D) + NORM_EPS) * qg_ref[...] * scale
        kn = k * lax.rsqrt(jnp.sum(k * k, axis=-1, keepdims=True) * (1.0 / D) + NORM_EPS) * kg_ref[...]
        q_aug = jnp.where(lane < D, qn,
                          jnp.where(lane < D + 3, 1.0,
                                    jnp.where(lane == D + 3, hi,
                                              jnp.where(lane == D + 4, mid,
                                                        jnp.where(lane == D + 5, lo, 0.0)))))
        k_aug = jnp.where(lane < D, kn,
                          jnp.where(lane == D, -hi,
                                    jnp.where(lane == D + 1, -mid,
                                              jnp.where(lane == D + 2, -lo,
                                                        jnp.where(lane < D + FOX_AUG, 1.0, 0.0)))))
        q_out[0, :, h * LANES:(h + 1) * LANES] = q_aug.astype(BF16)
        k_out[0, :, h * LANES:(h + 1) * LANES] = k_aug.astype(BF16)


def _fox_prep(qk, v, f, q_g, k_g, b_f, tb):
    B, T, _ = qk.shape
    HW = FOX_HEADS * LANES
    vec = pl.BlockSpec((1, LANES), lambda b, i: (0, 0))
    out_spec = pl.BlockSpec((1, tb, HW), lambda b, i: (b, i, 0))
    return pl.pallas_call(
        _fox_prep_kernel,
        grid=(B, T // tb),
        in_specs=[pl.BlockSpec((1, tb, 2 * HW), lambda b, i: (b, i, 0)),
                  pl.BlockSpec((1, tb, FOX_W), lambda b, i: (b, i, 0)),
                  pl.BlockSpec((1, tb, LANES), lambda b, i: (b, i, 0)),
                  vec, vec, vec],
        out_specs=[out_spec, out_spec,
                   pl.BlockSpec((1, FOX_HEADS, 1, LANES, tb), lambda b, i: (b, 0, i, 0, 0))],
        out_shape=[jax.ShapeDtypeStruct((B, T, HW), BF16)] * 2
        + [jax.ShapeDtypeStruct((B, FOX_HEADS, T // tb, LANES, tb), BF16)],
        scratch_shapes=[pltpu.VMEM((1, LANES), F32)],
        compiler_params=_cparams(("arbitrary", "arbitrary")),
        name="fox_prep",
    )(qk, v, f, q_g, k_g, b_f)


def _fox_attn_kernel(q_ref, k_ref, vt_ref, o_ref, m_ref, acc_ref, sa_ref, sb_ref, *, tq):
    qi = pl.program_id(2)
    m_ref[...] = jnp.full(m_ref.shape, NEG_BIG, F32)
    acc_ref[...] = jnp.zeros_like(acc_ref)

    tqh = tq // 2

    chains = [(hh, half) for hh in range(2) for half in range(2)]

    def scores(j, s_ref):
        start = pl.multiple_of(j * tq, tq)
        for c, (hh, half) in enumerate(chains):
            hs = slice(hh * LANES, (hh + 1) * LANES)
            qs = slice(half * tqh, (half + 1) * tqh)
            s_ref[c] = lax.dot_general(k_ref[0, pl.ds(start, tq), hs], q_ref[0, qs, hs],
                                       (((1,), (1,)), ((), ())),
                                       preferred_element_type=F32)

    def softmax_pv(j, s_ref, masked):
        for c, (hh, half) in enumerate(chains):
            st = s_ref[c]
            qs = slice(half * tqh, (half + 1) * tqh)
            if masked:
                keep = _iota(st.shape, 0) <= _iota(st.shape, 1) + half * tqh
                st = jnp.where(keep, st, NEG_BIG)
            m_old = m_ref[hh, :, qs]
            m_new = jnp.maximum(m_old, jnp.max(st, axis=0, keepdims=True))
            p = jnp.exp2(st - m_new).astype(BF16)
            alpha = jnp.exp2(m_old - m_new)
            m_ref[hh, :, qs] = m_new
            acc_ref[hh, :, qs] = alpha * acc_ref[hh, :, qs] + jnp.dot(vt_ref[0, hh, j], p,
                                                                       preferred_element_type=F32)

    def body(i, carry):
        j = 2 * i
        scores(j + 1, sb_ref)
        softmax_pv(j, sa_ref, False)
        scores(j + 2, sa_ref)
        softmax_pv(j + 1, sb_ref, False)
        return carry

    scores(0, sa_ref)
    lax.fori_loop(0, qi // 2, body, 0)

    @pl.when(qi % 2 == 0)
    def _():
        softmax_pv(qi, sa_ref, True)

    @pl.when(qi % 2 == 1)
    def _():
        scores(qi, sb_ref)
        softmax_pv(qi - 1, sa_ref, False)
        softmax_pv(qi, sb_ref, True)

    outs = []
    for hh in range(2):
        acc = acc_ref[hh]
        outs.append(acc[:HEAD_DIM, :] / acc[HEAD_DIM:HEAD_DIM + 1, :])
    o_ref[0] = jnp.concatenate(outs, axis=0).T


def _fox_attn(q_aug, k_aug, vt, tq):
    B, T, _ = q_aug.shape
    npairs = FOX_HEADS // 2
    return pl.pallas_call(
        functools.partial(_fox_attn_kernel, tq=tq),
        grid=(B, npairs, T // tq),
        in_specs=[pl.BlockSpec((1, tq, 2 * LANES), lambda b, h, i: (b, i, h)),
                  pl.BlockSpec((1, T, 2 * LANES), lambda b, h, i: (b, 0, h)),
                  pl.BlockSpec((1, 2, T // tq, LANES, tq), lambda b, h, i: (b, h, 0, 0, 0))],
        out_specs=pl.BlockSpec((1, tq, LANES), lambda b, h, i: (b, i, h)),
        out_shape=jax.ShapeDtypeStruct((B, T, FOX_W), F32),
        scratch_shapes=[pltpu.VMEM((2, 1, tq), F32), pltpu.VMEM((2, LANES, tq), F32),
                        pltpu.VMEM((4, tq, tq // 2), F32), pltpu.VMEM((4, tq, tq // 2), F32)],
        compiler_params=_cparams(("parallel", "parallel", "arbitrary")),
        name="fox_attn",
    )(q_aug, k_aug, vt)


def _gelu_tanh(x):
    return 0.5 * x * (1.0 + jnp.tanh(math.sqrt(2.0 / math.pi) * (x + 0.044715 * (x * x * x))))


def _mix_out_kernel(h_ref, yr_ref, ys_ref, us_ref, of_ref, og_ref, mk_ref, mv_ref,
                    d_ref, bglu_ref, sg_ref, fg_ref, xg_ref,
                    wglu_ref, wout_ref, wxq_ref, wxo_ref, o_ref):
    h = h_ref[...]
    ys = _gelu_tanh(ys_ref[...] + d_ref[...] * us_ref[...])
    ys = ys * _sigmoid(_bdot(ys, wglu_ref[...]) + bglu_ref[...])
    ys = _rms(ys, sg_ref[...])
    yf = _rms(of_ref[...] * _sigmoid(og_ref[...]), fg_ref[...])
    r0, r1 = RWKV_W, RWKV_W + S5_W
    h = h + (_bdot(yr_ref[...], wout_ref[0:r0, :]) + _bdot(ys, wout_ref[r0:r1, :])
             + _bdot(yf, wout_ref[r1:, :]))
    q = jnp.dot(_rms(h, xg_ref[...]).astype(BF16), wxq_ref[...], preferred_element_type=F32)
    scale = 1.0 / math.sqrt(XATTN_HEAD_DIM)
    upd = jnp.zeros_like(h)
    for hd in range(XATTN_HEADS):
        cs = slice(hd * XATTN_HEAD_DIM, (hd + 1) * XATTN_HEAD_DIM)
        s = _bdot_nt(q[:, cs], mk_ref[0, :, cs]) * scale
        s = s - jnp.max(s, axis=-1, keepdims=True)
        p = jnp.exp(s)
        p = p / jnp.sum(p, axis=-1, keepdims=True)
        o = _bdot(p, mv_ref[0, :, cs])
        upd = upd + _bdot(o, wxo_ref[cs, :])
    o_ref[...] = h + upd


def _mix_out(h, yr, ys, us, of, og, mem_k, mem_v, d, bglu, sg, fg, xg, wglu, wout, wxq, wxo,
             rows_per_batch, tm=1024):
    n, dm = h.shape
    tm = min(tm, rows_per_batch)
    assert rows_per_batch % tm == 0
    bpb = rows_per_batch // tm
    rowblk = lambda w: pl.BlockSpec((tm, w), lambda i: (i, 0))
    vec = lambda w: pl.BlockSpec((1, w), lambda i: (0, 0))
    memblk = pl.BlockSpec((1,) + mem_k.shape[1:], lambda i: (i // bpb, 0, 0))
    return pl.pallas_call(
        _mix_out_kernel,
        grid=(n // tm,),
        in_specs=[rowblk(dm), rowblk(RWKV_W), rowblk(S5_W), rowblk(S5_W), rowblk(FOX_W), rowblk(FOX_W),
                  memblk, memblk,
                  vec(S5_W), vec(S5_W), vec(S5_W), vec(FOX_W), vec(dm),
                  _vmem_full(), _vmem_full(), _vmem_full(), _vmem_full()],
        out_specs=rowblk(dm),
        out_shape=jax.ShapeDtypeStruct((n, dm), F32),
        compiler_params=_cparams(("parallel",)),
        name="mix_out",
    )(h, yr, ys, us, of, og, mem_k, mem_v, d, bglu, sg, fg, xg, wglu, wout, wxq, wxo)


def _ffn_kernel(h_ref, g_ref, w1_ref, w2_ref, fg_ref, o_ref, *, ff_chunk, final_norm):
    h = h_ref[...]
    hn = _rms(h, g_ref[...]).astype(BF16)
    acc = h
    for c0 in range(0, w1_ref.shape[1], ff_chunk):
        z = jnp.dot(hn, w1_ref[:, c0:c0 + ff_chunk], preferred_element_type=F32)
        z = jnp.maximum(z, 0.0)
        acc = acc + jnp.dot((z * z).astype(BF16), w2_ref[c0:c0 + ff_chunk, :], preferred_element_type=F32)
    if final_norm:
        acc = _rms(acc, fg_ref[...])
    o_ref[...] = acc


def _ffn(h, g, w1, w2, fg, final_norm, tm=512):
    n, dm = h.shape
    tm = min(tm, n)
    assert n % tm == 0
    vec = pl.BlockSpec((1, dm), lambda i: (0, 0))
    return pl.pallas_call(
        functools.partial(_ffn_kernel, ff_chunk=1024, final_norm=final_norm),
        grid=(n // tm,),
        in_specs=[pl.BlockSpec((tm, dm), lambda i: (i, 0)), vec, _vmem_full(), _vmem_full(), vec],
        out_specs=pl.BlockSpec((tm, dm), lambda i: (i, 0)),
        out_shape=jax.ShapeDtypeStruct((n, dm), F32),
        compiler_params=_cparams(("parallel",)),
        name="ffn",
    )(h, g.reshape(1, dm), w1, w2, fg.reshape(1, dm))


def _head_block_ones(width):
    i = jnp.arange(width) // HEAD_DIM
    return (i[:, None] == i[None, :]).astype(BF16)


def _rwkv_reorder(x):
    W = RWKV_W
    o_k = W + DECAY_LORA
    o_v = o_k + W
    o_a = o_v + W
    return jnp.concatenate([x[..., :W], x[..., o_k:o_v], x[..., o_v:o_a], x[..., W:o_k], x[..., o_a:]], axis=-1)


def _pad_heads(w, nheads):
    d = w.shape[0]
    w = w.reshape(d, nheads, HEAD_DIM)
    w = jnp.pad(w, ((0, 0), (0, 0), (0, LANES - HEAD_DIM)))
    return w.reshape(d, nheads * LANES)


def _pad_lanes(v, width=LANES):
    return jnp.pad(v, (0, width - v.shape[0])).reshape(1, width)


_QK_W = 2 * FOX_HEADS * LANES
_COLS = {}
_c = 0
for _name, _w in (("rwkv", RWKV_IN), ("s5", S5_W), ("qk", _QK_W), ("v", FOX_W), ("og", FOX_W), ("f", LANES)):
    _COLS[_name] = (_c, _c + _w)
    _c += _w
_COL_ORDER = ("rwkv", "s5", "qk", "v", "og", "f")


def _layout_w_in(w_in):
    o = RWKV_IN + S5_W
    w_r = _rwkv_reorder(w_in[:, :RWKV_IN])
    w_s = w_in[:, RWKV_IN:o]
    w_q = _pad_heads(w_in[:, o:o + FOX_W], FOX_HEADS)
    w_k = _pad_heads(w_in[:, o + FOX_W:o + 2 * FOX_W], FOX_HEADS)
    w_v = w_in[:, o + 2 * FOX_W:o + 3 * FOX_W]
    w_g = w_in[:, o + 3 * FOX_W:o + 4 * FOX_W]
    w_f = jnp.pad(w_in[:, o + 4 * FOX_W:], ((0, 0), (0, LANES - FOX_HEADS)))
    return jnp.concatenate([w_r, w_s, w_q, w_k, w_v, w_g, w_f], axis=1).astype(BF16)


def kernel(x, mem, mem_norm_g, w_mem_kv, mix_norm_g, w_in, rwkv_mu, rwkv_w0, rwkv_w2, rwkv_a0, rwkv_a2, rwkv_g2, rwkv_k_k, rwkv_k_a, rwkv_r_k, rwkv_ln_w, rwkv_ln_b, s5_lam_re, s5_lam_im, s5_log_dt, s5_b_re, s5_b_im, s5_c_re, s5_c_im, s5_d, s5_w_glu, s5_b_glu, s5_out_g, fox_q_g, fox_k_g, fox_b_f, fox_out_g, w_out, xattn_norm_g, w_xq, w_xo, ffn_norm_g, w_ffn1, w_ffn2, final_norm_g):
    B, T, D = x.shape
    M = mem.shape[1]
    N = B * T
    depth = w_in.shape[0]
    W = RWKV_W
    nchunks = T // S5_CHUNK
    e_head = _head_block_ones(W)

    mem_kv = _norm_matmul(mem.reshape(B * M, D), mem_norm_g, w_mem_kv.astype(BF16),
                          [(0, D), (D, 2 * D)], [BF16, BF16], tm=256)
    mem_k = mem_kv[0].reshape(B, M, D)
    mem_v = mem_kv[1].reshape(B, M, D)

    col_ranges = [_COLS[n] for n in _COL_ORDER]
    col_dtypes = [F32] * len(col_ranges)
    h = x.reshape(N, D)
    for l in range(depth):
        p_r, p_s, p_qk, p_v, p_og, p_f = _norm_matmul(
            h, mix_norm_g[l], _layout_w_in(w_in[l]), col_ranges, col_dtypes)

        wwa = jnp.zeros((LANES, 2 * W), F32)
        wwa = wwa.at[:DECAY_LORA, :W].set(rwkv_w2[l]).at[DECAY_LORA:, W:].set(rwkv_a2[l]).astype(BF16)
        row = lambda v: v.reshape(1, -1)
        r, k, v, an, b, ld, g = _rwkv_prep(
            p_r.reshape(B, T, RWKV_IN), row(_rwkv_reorder(rwkv_mu[l])), row(rwkv_w0[l]), row(rwkv_a0[l]),
            row(rwkv_k_k[l]), row(rwkv_k_a[l]), wwa, rwkv_g2[l].astype(BF16), e_head)
        y_r = _rwkv_scan(r, k, v, an, b, ld, g, row(rwkv_ln_w[l]), row(rwkv_ln_b[l]),
                         row(rwkv_r_k[l])).reshape(N, W)

        m1, w1, w2, a1, a2 = _s5_weights(s5_lam_re[l], s5_lam_im[l], s5_log_dt[l], s5_b_re[l], s5_b_im[l],
                                         s5_c_re[l], s5_c_im[l], nchunks)
        u_t = p_s.reshape(B * nchunks, S5_CHUNK, S5_GROUPS, S5_GROUP_CH).transpose(2, 0, 1, 3)
        u_t = u_t.reshape(S5_GROUPS, B * nchunks, S5_CHUNK * S5_GROUP_CH)
        y_t = _s5_conv(u_t, m1, w1, w2, a1, a2, nchunks)
        y_s = y_t.reshape(S5_GROUPS, B * nchunks, S5_CHUNK, S5_GROUP_CH).transpose(1, 2, 0, 3).reshape(N, S5_W)

        fox_blk = min(FOX_BLOCK, T)
        q_aug, k_aug, v_t = _fox_prep(p_qk.reshape(B, T, _QK_W), p_v.reshape(B, T, FOX_W),
                                      p_f.reshape(B, T, LANES), _pad_lanes(fox_q_g[l]),
                                      _pad_lanes(fox_k_g[l]), _pad_lanes(fox_b_f[l]), fox_blk)
        o_f = _fox_attn(q_aug, k_aug, v_t, fox_blk).reshape(N, FOX_W)

        h = _mix_out(h, y_r, y_s, p_s, o_f, p_og, mem_k, mem_v,
                     row(s5_d[l]), row(s5_b_glu[l]), row(s5_out_g[l]), row(fox_out_g[l]), row(xattn_norm_g[l]),
                     s5_w_glu[l].astype(BF16), w_out[l].astype(BF16), w_xq[l].astype(BF16),
                     w_xo[l].astype(BF16), rows_per_batch=T)
        h = _ffn(h, ffn_norm_g[l], w_ffn1[l].astype(BF16), w_ffn2[l].astype(BF16), final_norm_g,
                 final_norm=(l == depth - 1))
    return h.reshape(B, T, D)
```
